```python
import math
import jax, jax.numpy as jnp
from jax import lax
import numpy as np

D_MODEL = 1024
BATCH = 8
SEQ = 4096
DEPTH = 2
DEC_BATCH = 8
DEC_SEQ = 32
PAST_LEN = 2048

CHUNK = 64
HEAD_DIM = 64
N_HEADS_A = 8
N_HEADS_B = 8
WIDTH_A = N_HEADS_A * HEAD_DIM
WIDTH_B = N_HEADS_B * HEAD_DIM
MIX_WIDTH = WIDTH_A + WIDTH_B
IDX_HEADS = 8
IDX_DIM = 64
TOPK_MAX = 256
Q_BLOCK = 128
N_BUCKETS = 32
T5_MAX_DISTANCE = 128
D_FF = 2816
CONV_WIDTH = 3
EPS = 1e-6
PROJ_SPLITS = (WIDTH_A, WIDTH_A, WIDTH_A, WIDTH_B, WIDTH_B, WIDTH_B, IDX_HEADS * IDX_DIM, IDX_DIM, IDX_HEADS)
PROJ_WIDTH = 3 * WIDTH_A + 3 * WIDTH_B + IDX_HEADS * IDX_DIM + IDX_DIM + IDX_HEADS

kernel_name = "hybrid_stickbreaking_dsa_convffn_step"


def rms_norm(x, g):
    xf = x.astype(jnp.float32)
    y = xf * lax.rsqrt(jnp.mean(xf * xf, axis=-1, keepdims=True) + EPS)
    return (y * g.astype(jnp.float32)).astype(x.dtype)


def split_cols(a, sizes):
    out, start = [], 0
    for s in sizes:
        out.append(a[..., start:start + s])
        start += s
    return out


def to_blocks(a, bs):
    B, T = a.shape[:2]
    return a.reshape((B, T // bs, bs) + a.shape[2:]).swapaxes(0, 1)


def from_blocks(a):
    a = a.swapaxes(0, 1)
    return a.reshape((a.shape[0], a.shape[1] * a.shape[2]) + a.shape[3:])


def t5_bucket(rel):
    half = N_BUCKETS // 2
    max_exact = half // 2
    base = jnp.where(rel > 0, half, 0)
    n = jnp.abs(rel)
    large = max_exact + (jnp.log(jnp.maximum(n, 1).astype(jnp.float32) / max_exact)
                         / math.log(T5_MAX_DISTANCE / max_exact) * (half - max_exact)).astype(jnp.int32)
    large = jnp.minimum(large, half - 1)
    return base + jnp.where(n < max_exact, n, large)


def stick_breaking(q, k, v, q_pos, k_pos):
    B, T, H, dh = q.shape
    bs = min(Q_BLOCK, T)
    scale = dh ** -0.5

    def block(args):
        qb, pb = args
        z = jnp.einsum('bqhd,bkhd->bhqk', qb, k).astype(jnp.float32) * scale
        causal = (k_pos[None, :] < pb[:, None])[None, None]
        log_beta = jax.nn.log_sigmoid(z)
        log_1mb = jnp.where(causal, jax.nn.log_sigmoid(-z), 0.0)
        between = lax.cumsum(log_1mb, axis=3, reverse=True) - log_1mb
        w = jnp.where(causal, jnp.exp(log_beta + between), 0.0)
        return jnp.einsum('bhqk,bkhd->bqhd', w.astype(v.dtype), v)

    out = lax.map(block, (to_blocks(q, bs), q_pos.reshape(T // bs, bs)))
    return from_blocks(out)


def indexed_sparse_attention(q, k, v, qi, ki, wi, q_pos, k_pos, rel_bias, top_k):
    B, T, H, dh = q.shape
    L = k.shape[1]
    bs = min(Q_BLOCK, T)
    kf = k.reshape(B, L, H * dh)
    vf = v.reshape(B, L, H * dh)
    gather = jax.vmap(lambda a, i: a[i])

    def block(args):
        qb, qib, wib, pb = args
        s = jax.nn.relu(jnp.einsum('bqhd,bkd->bqhk', qib, ki).astype(jnp.float32) * IDX_DIM ** -0.5)
        score = jnp.einsum('bqhk,bqh->bqk', s, wib.astype(jnp.float32) * IDX_HEADS ** -0.5)
        admissible = (k_pos[None, :] // CHUNK) <= (pb[:, None] // CHUNK)
        score = jnp.where(admissible[None], score, -jnp.inf)
        _, idx = lax.top_k(score, top_k)
        sel_pos = k_pos[idx]
        valid = (sel_pos // CHUNK) <= (pb[None, :, None] // CHUNK)
        ks = gather(kf, idx).reshape(B, bs, top_k, H, dh)
        vs = gather(vf, idx).reshape(B, bs, top_k, H, dh)
        logits = jnp.einsum('bqhd,bqkhd->bqhk', qb, ks).astype(jnp.float32) * dh ** -0.5
        bias = rel_bias[t5_bucket(sel_pos - pb[None, :, None])].astype(jnp.float32)
        logits = logits + jnp.swapaxes(bias, 2, 3)
        logits = jnp.where(valid[:, :, None, :], logits, -jnp.inf)
        p = jax.nn.softmax(logits, axis=-1)
        return jnp.einsum('bqhk,bqkhd->bqhd', p.astype(vs.dtype), vs)

    out = lax.map(block, (to_blocks(q, bs), to_blocks(qi, bs), to_blocks(wi, bs), q_pos.reshape(T // bs, bs)))
    return from_blocks(out)


def conv_ffn(h, conv_state, w_up, conv_w, conv_b, w_down):
    T = h.shape[1]
    u = h @ w_up
    ext = jnp.concatenate([conv_state, u], axis=1)
    c = conv_b
    for i in range(CONV_WIDTH):
        c = c + ext[:, i:i + T] * conv_w[i]
    a, g = jnp.split(c, 2, axis=-1)
    y = (jax.nn.silu(g) * a) @ w_down
    return y, ext[:, ext.shape[1] - (CONV_WIDTH - 1):]


def trunk_layer(x, q_pos, caches, top_k, rel_bias, norm1, w_in, q_norm, k_norm, w_out, norm2, w_up, conv_w, conv_b, w_down):
    B, T, _ = x.shape
    h = rms_norm(x, norm1)
    a_q, a_k, a_v, b_q, b_k, b_v, i_q, i_k, i_w = split_cols(h @ w_in, PROJ_SPLITS)
    a_q = a_q.reshape(B, T, N_HEADS_A, HEAD_DIM)
    a_k = a_k.reshape(B, T, N_HEADS_A, HEAD_DIM)
    a_v = a_v.reshape(B, T, N_HEADS_A, HEAD_DIM)
    b_q = rms_norm(b_q.reshape(B, T, N_HEADS_B, HEAD_DIM), q_norm)
    b_k = rms_norm(b_k.reshape(B, T, N_HEADS_B, HEAD_DIM), k_norm)
    b_v = b_v.reshape(B, T, N_HEADS_B, HEAD_DIM)
    i_q = i_q.reshape(B, T, IDX_HEADS, IDX_DIM)
    new_rows = (a_k, a_v, b_k, b_v, i_k)
    if caches is None:
        ka, va, kb, vb, ki = new_rows
        conv_state = jnp.zeros((B, CONV_WIDTH - 1, 2 * D_FF), x.dtype)
    else:
        ka, va, kb, vb, ki = [jnp.concatenate([c, r], axis=1) for c, r in zip(caches[:5], new_rows)]
        conv_state = caches[5]
    k_pos = jnp.arange(ka.shape[1], dtype=jnp.int32)
    o_a = stick_breaking(a_q, ka, va, q_pos, k_pos)
    o_b = indexed_sparse_attention(b_q, kb, vb, i_q, ki, i_w, q_pos, k_pos, rel_bias, top_k)
    x = x + jnp.concatenate([o_a.reshape(B, T, WIDTH_A), o_b.reshape(B, T, WIDTH_B)], axis=-1) @ w_out
    f, conv_new = conv_ffn(rms_norm(x, norm2), conv_state, w_up, conv_w, conv_b, w_down)
    return x + f, new_rows + (conv_new,)


def setup_inputs(seed: int = 0) -> dict:
    key = jax.random.key(seed)
    ks = jax.random.split(key, 20)
    f32 = jnp.float32
    nrm = lambda k, shape, s=1.0: jax.random.normal(k, shape, f32) * s
    return {
        "x_prompt": nrm(ks[0], (BATCH, SEQ, D_MODEL)),
        "x_sample": nrm(ks[1], (DEC_BATCH, DEC_SEQ, D_MODEL)),
        "cache_a_k": nrm(ks[2], (DEPTH, DEC_BATCH, PAST_LEN, N_HEADS_A, HEAD_DIM)),
        "cache_a_v": nrm(ks[3], (DEPTH, DEC_BATCH, PAST_LEN, N_HEADS_A, HEAD_DIM)),
        "cache_b_k": nrm(ks[4], (DEPTH, DEC_BATCH, PAST_LEN, N_HEADS_B, HEAD_DIM)),
        "cache_b_v": nrm(ks[5], (DEPTH, DEC_BATCH, PAST_LEN, N_HEADS_B, HEAD_DIM)),
        "cache_idx_k": nrm(ks[6], (DEPTH, DEC_BATCH, PAST_LEN, IDX_DIM)),
        "state_ffn_conv": nrm(ks[7], (DEPTH, DEC_BATCH, CONV_WIDTH - 1, 2 * D_FF)),
        "rel_bias": nrm(ks[8], (N_BUCKETS, N_HEADS_B), 0.5),
        "norm1": 1.0 + nrm(ks[9], (DEPTH, D_MODEL), 0.01),
        "w_in": nrm(ks[10], (DEPTH, D_MODEL, PROJ_WIDTH), D_MODEL ** -0.5),
        "q_norm": 1.0 + nrm(ks[11], (DEPTH, HEAD_DIM), 0.01),
        "k_norm": 1.0 + nrm(ks[12], (DEPTH, HEAD_DIM), 0.01),
        "w_out": nrm(ks[13], (DEPTH, MIX_WIDTH, D_MODEL), MIX_WIDTH ** -0.5),
        "norm2": 1.0 + nrm(ks[14], (DEPTH, D_MODEL), 0.01),
        "w_up": nrm(ks[15], (DEPTH, D_MODEL, 2 * D_FF), D_MODEL ** -0.5),
        "conv_w": nrm(ks[16], (DEPTH, CONV_WIDTH, 2 * D_FF), CONV_WIDTH ** -0.5),
        "conv_b": nrm(ks[17], (DEPTH, 2 * D_FF), 0.01),
        "w_down": nrm(ks[18], (DEPTH, D_FF, D_MODEL), D_FF ** -0.5),
    }


def reference(x_prompt, x_sample, cache_a_k, cache_a_v, cache_b_k, cache_b_v, cache_idx_k, state_ffn_conv,
              rel_bias, norm1, w_in, q_norm, k_norm, w_out, norm2, w_up, conv_w, conv_b, w_down):
    t_p = x_prompt.shape[1]
    t_s = x_sample.shape[1]
    past = cache_a_k.shape[2]
    pos_p = jnp.arange(t_p, dtype=jnp.int32)
    pos_s = past + jnp.arange(t_s, dtype=jnp.int32)
    topk_p = min(TOPK_MAX, t_p // 4)
    topk_s = min(TOPK_MAX, (past + t_s) // 4)
    yp, ys = x_prompt, x_sample
    p_states, s_states = [], []
    for l in range(DEPTH):
        lw = (rel_bias, norm1[l], w_in[l], q_norm[l], k_norm[l], w_out[l], norm2[l], w_up[l], conv_w[l], conv_b[l], w_down[l])
        yp, st_p = trunk_layer(yp, pos_p, None, topk_p, *lw)
        caches = (cache_a_k[l], cache_a_v[l], cache_b_k[l], cache_b_v[l], cache_idx_k[l], state_ffn_conv[l])
        ys, st_s = trunk_layer(ys, pos_s, caches, topk_s, *lw)
        p_states.append(st_p)
        s_states.append(st_s)
    stk = lambda sts, i: jnp.stack([s[i] for s in sts], axis=0)
    p_a_k, p_a_v, p_b_k, p_b_v, p_idx_k, p_conv = [stk(p_states, i) for i in range(6)]
    s_a_k, s_a_v, s_b_k, s_b_v, s_idx_k, s_conv = [stk(s_states, i) for i in range(6)]
    return (yp, ys, p_a_k, p_a_v, p_b_k, p_b_v, p_idx_k, p_conv, s_a_k, s_a_v, s_b_k, s_b_v, s_idx_k, s_conv)
```

```python
import functools
import math

import jax
import jax.numpy as jnp
from jax import lax
from jax.experimental import pallas as pl
from jax.experimental.pallas import tpu as pltpu

HEAD_DIM = 64
N_HEADS = 8
IDX_DIM = 64
WIDTH = N_HEADS * HEAD_DIM
CHUNK = 64
CHUNK_SHIFT = 6
TOPK_MAX = 256
N_BUCKETS = 32
T5_MAX_DISTANCE = 128
CONV_WIDTH = 3
EPS = 1e-6

LANES = 128
SUBLANES = 8
Q_TILE = 128
KEY_TILE = 256
ROW_TILE = 256
FF_TILE = 256
VMEM_LIMIT_BYTES = 56 * 1024 * 1024

MXU_DTYPE = jnp.bfloat16
F32 = jnp.float32
I32 = jnp.int32
INT_MIN = -(2 ** 31)
NEG = -1e30

_NT = (((1,), (1,)), ((), ()))


def _dot(a, b):
    return jnp.dot(a, b, preferred_element_type=F32)


def _dot_nt(a, b):
    return lax.dot_general(a, b, _NT, preferred_element_type=F32)


def _split_hi_lo(x):
    hi = x.astype(MXU_DTYPE)
    lo = (x - hi.astype(F32)).astype(MXU_DTYPE)
    return jnp.concatenate([hi, lo], axis=1)


def _params(semantics):
    return pltpu.CompilerParams(dimension_semantics=semantics, vmem_limit_bytes=VMEM_LIMIT_BYTES)


def _in_proj_kernel(x_ref, g1_ref, w_ref, wt_ref, gq_ref, gk_ref, seg_ref,
                    ak_ref, av_ref, bk_ref, bv_ref, ik_ref,
                    qa_ref, ka_ref, va_ref, qb_ref, kb_ref, vb_ref, qi_ref, kid_ref, wi_ref):
    x = x_ref[...]
    ms = jnp.mean(x * x, axis=-1, keepdims=True)
    h = (x * lax.rsqrt(ms + EPS) * g1_ref[...]).astype(MXU_DTYPE)

    def proj(j):
        return _dot(h, w_ref[:, j * WIDTH:(j + 1) * WIDTH])

    def head_norm(p, g_ref):
        ss = _dot(_split_hi_lo(p * p), seg_ref[...])
        return p * lax.rsqrt(ss * (1.0 / HEAD_DIM) + EPS) * g_ref[...]

    scale = HEAD_DIM ** -0.5
    qa_ref[...] = (proj(0) * scale).astype(MXU_DTYPE)
    p = proj(1)
    ak_ref[...] = p
    ka_ref[...] = p.astype(MXU_DTYPE)
    p = proj(2)
    av_ref[...] = p
    va_ref[...] = p.astype(MXU_DTYPE)
    qb_ref[...] = (head_norm(proj(3), gq_ref) * scale).astype(MXU_DTYPE)
    p = head_norm(proj(4), gk_ref)
    bk_ref[...] = p
    kb_ref[...] = p.astype(MXU_DTYPE)
    p = proj(5)
    bv_ref[...] = p
    vb_ref[...] = p.astype(MXU_DTYPE)
    qi_ref[...] = proj(6).astype(MXU_DTYPE)
    t = _dot(h, wt_ref[...])
    ik_ref[...] = t[:, :IDX_DIM]
    kid_ref[...] = t[:, :LANES].astype(MXU_DTYPE)
    wi_ref[...] = t[:, LANES:] * (IDX_DIM ** -0.5 * N_HEADS ** -0.5)


def _in_proj(x2d, g1, w_main, w_tail, gq, gk, seg):
    rows, d = x2d.shape
    tm = min(ROW_TILE, rows)
    assert rows % tm == 0
    row_spec = lambda w: pl.BlockSpec((tm, w), lambda i: (i, 0))
    full = lambda a: pl.BlockSpec(a.shape, lambda i: (0,) * a.ndim)
    f32_out = lambda w: jax.ShapeDtypeStruct((rows, w), F32)
    mx_out = lambda w: jax.ShapeDtypeStruct((rows, w), MXU_DTYPE)
    return pl.pallas_call(
        _in_proj_kernel,
        grid=(rows // tm,),
        in_specs=[row_spec(d), full(g1), full(w_main), full(w_tail), full(gq), full(gk), full(seg)],
        out_specs=[row_spec(WIDTH)] * 4 + [row_spec(IDX_DIM)] + [row_spec(WIDTH)] * 7
                  + [row_spec(LANES), row_spec(LANES)],
        out_shape=[f32_out(WIDTH)] * 4 + [f32_out(IDX_DIM)] + [mx_out(WIDTH)] * 7
                  + [mx_out(LANES), f32_out(LANES)],
        compiler_params=_params(("parallel",)),
        name="in_proj",
    )(x2d, g1, w_main, w_tail, gq, gk, seg)


def _stick_breaking_kernel(q_ref, k_ref, v_ref, tri_ref, o_ref, *, q_off, tq):
    qi = pl.program_id(2)
    p0 = q_off + qi * tq
    q = q_ref[...]
    lane = lax.broadcasted_iota(I32, (1, LANES), 1)
    row_pos = p0 + lax.broadcasted_iota(I32, (tq, 1), 0)
    q_heads = [jnp.where(lane < HEAD_DIM, q, 0), jnp.where(lane >= HEAD_DIM, q, 0)]
    n_blocks = (p0 + tq - 1 + LANES - 1) // LANES
    tri = tri_ref[...]

    def body(i, carry):
        kb = n_blocks - 1 - i
        start = pl.multiple_of(kb * LANES, LANES)
        kt = k_ref[pl.ds(start, LANES), :]
        vt = v_ref[pl.ds(start, LANES), :]
        causal = (start + lane) < row_pos
        out = []
        for hh in range(2):
            acc, later = carry[2 * hh], carry[2 * hh + 1]
            z = _dot_nt(q_heads[hh], kt)
            t = jnp.log(1.0 + jnp.exp(-jnp.abs(z)))
            log_beta = jnp.minimum(z, 0.0) - t
            log_1mb = jnp.where(causal, -jnp.maximum(z, 0.0) - t, 0.0)
            within = _dot(_split_hi_lo(log_1mb), tri)
            w = jnp.where(causal, jnp.exp(log_beta + (within + later)), 0.0)
            acc = acc + _dot(w.astype(MXU_DTYPE), vt)
            later = later + jnp.sum(log_1mb, axis=1, keepdims=True)
            out += [acc, later]
        return tuple(out)

    zero_acc = jnp.zeros((tq, LANES), F32)
    zero_col = jnp.zeros((tq, 1), F32)
    acc0, _, acc1, _ = lax.fori_loop(0, n_blocks, body, (zero_acc, zero_col, zero_acc, zero_col))
    o_ref[...] = jnp.where(lane < HEAD_DIM, acc0, acc1).astype(o_ref.dtype)


def _stick_breaking(q, k, v, tri, q_off):
    b, t, _ = q.shape
    lp = k.shape[1]
    tq = min(Q_TILE, t)
    assert t % tq == 0 and lp % LANES == 0
    pairs = WIDTH // LANES
    kv_spec = pl.BlockSpec((None, lp, LANES), lambda bi, pi, qi: (bi, 0, pi))
    q_spec = pl.BlockSpec((None, tq, LANES), lambda bi, pi, qi: (bi, qi, pi))
    return pl.pallas_call(
        functools.partial(_stick_breaking_kernel, q_off=q_off, tq=tq),
        grid=(b, pairs, t // tq),
        in_specs=[q_spec, kv_spec, kv_spec, pl.BlockSpec(tri.shape, lambda bi, pi, qi: (0, 0))],
        out_specs=q_spec,
        out_shape=jax.ShapeDtypeStruct(q.shape, MXU_DTYPE),
        compiler_params=_params(("parallel", "parallel", "arbitrary")),
        name="stick_breaking",
    )(q, k, v, tri)


def _bias_tiles_kernel(rb_ref, bucket_ref, tb_ref):
    h = pl.program_id(0)
    for d in range(bucket_ref.shape[0]):
        bucket = bucket_ref[d]
        acc = jnp.zeros(bucket.shape, F32)
        for j in range(N_BUCKETS):
            acc = jnp.where(bucket == j, rb_ref[j, h], acc)
        tb_ref[d] = acc


def _t5_bucket(rel):
    half = N_BUCKETS // 2
    max_exact = half // 2
    base = jnp.where(rel > 0, half, 0)
    n = jnp.abs(rel)
    large = max_exact + (jnp.log(jnp.maximum(n, 1).astype(jnp.float32) / max_exact)
                         / math.log(T5_MAX_DISTANCE / max_exact) * (half - max_exact)).astype(jnp.int32)
    large = jnp.minimum(large, half - 1)
    return base + jnp.where(n < max_exact, n, large)


N_BIAS_TILES = 3


def _bias_tiles(rel_bias):
    assert T5_MAX_DISTANCE <= LANES
    i = jnp.arange(LANES, dtype=I32)[None, :, None]
    j = jnp.arange(LANES, dtype=I32)[None, None, :]
    d = jnp.arange(N_BIAS_TILES, dtype=I32)[:, None, None]
    bucket = _t5_bucket(j - i - LANES * d).astype(I32)
    return pl.pallas_call(
        _bias_tiles_kernel,
        grid=(N_HEADS,),
        in_specs=[pl.BlockSpec(memory_space=pltpu.SMEM),
                  pl.BlockSpec(bucket.shape, lambda h: (0, 0, 0))],
        out_specs=pl.BlockSpec((N_BIAS_TILES, LANES, LANES), lambda h: (h, 0, 0)),
        out_shape=jax.ShapeDtypeStruct((N_HEADS * N_BIAS_TILES, LANES, LANES), F32),
        compiler_params=_params(("arbitrary",)),
        name="bias_tiles",
    )(rel_bias, bucket)


def _sparse_kernel(q_ref, k_ref, v_ref, qi_ref, kid_ref, wi_ref, tb_ref, o_ref,
                   key_ref, mb_ref, cut_ref, *, q_off, tq, n_keys, top_k):
    kt_w = KEY_TILE
    p0 = q_off + pl.program_id(1) * tq
    lane = lax.broadcasted_iota(I32, (1, LANES), 1)
    col = lax.broadcasted_iota(I32, (1, kt_w), 1)
    row_pos = p0 + lax.broadcasted_iota(I32, (tq, 1), 0)
    row_chunk = lax.shift_right_arithmetic(row_pos, CHUNK_SHIFT)
    adm_end = (lax.shift_right_arithmetic(p0 + tq - 1, CHUNK_SHIFT) + 1) * CHUNK
    n_blocks = (jnp.minimum(adm_end, n_keys) + kt_w - 1) // kt_w
    half_masks = [lane < HEAD_DIM, lane >= HEAD_DIM]

    def admissible(start):
        cols = start + col
        chunk_ok = lax.shift_right_arithmetic(cols, CHUNK_SHIFT) <= row_chunk
        return chunk_ok & (cols < n_keys)

    wi = wi_ref[...]
    qi_heads = []
    for h in range(N_HEADS):
        pair = qi_ref[:, (h // 2) * LANES:(h // 2 + 1) * LANES]
        qi_heads.append(jnp.where(half_masks[h % 2], pair, 0))

    def score_block(kb, _):
        start = pl.multiple_of(kb * kt_w, kt_w)
        kt = kid_ref[pl.ds(start, kt_w), :]
        acc = jnp.zeros((tq, kt_w), F32)
        for h in range(N_HEADS):
            acc = acc + jnp.maximum(_dot_nt(qi_heads[h], kt), 0.0) * wi[:, h:h + 1]
        acc = jnp.where(acc == 0.0, 0.0, acc)
        bits = lax.bitcast_convert_type(acc, I32)
        key = jnp.where(bits < 0, bits ^ 0x7FFFFFFF, bits)
        key_ref[kb] = jnp.where(admissible(start), key, INT_MIN)
        return 0

    lax.fori_loop(0, n_blocks, score_block, 0)

    def count(pred):
        def blk(kb, cnt):
            return cnt + jnp.where(pred(key_ref[kb], kb), 1.0, 0.0)
        cnt = lax.fori_loop(0, n_blocks, blk, jnp.zeros((tq, kt_w), F32))
        return jnp.sum(cnt, axis=1, keepdims=True)

    def count_ge(cand):
        cand_b = jnp.broadcast_to(cand, (tq, kt_w))
        return count(lambda key, kb: key >= cand_b)

    k_f = float(top_k)
    thr = jnp.where(count_ge(jnp.zeros((tq, 1), I32)) >= k_f, 0, INT_MIN).astype(I32)

    def bit_step(i, thr):
        cand = thr | lax.shift_left(jnp.int32(1), 30 - i)
        return jnp.where(count_ge(cand) >= k_f, cand, thr)

    thr = lax.fori_loop(0, 31, bit_step, thr)
    thr_b = jnp.broadcast_to(thr, (tq, kt_w))

    n_ge = count_ge(thr)
    n_cols = key_ref.shape[0] * kt_w
    cut_ref[...] = jnp.full(cut_ref.shape, n_cols, I32)

    @pl.when(jnp.max(n_ge) > k_f)
    def _():
        need = k_f - count(lambda key, kb: key > thr_b)

        def cut_step(i, cut):
            cand = cut | lax.shift_left(jnp.int32(1), n_cols.bit_length() - 1 - i)
            cand_b = jnp.broadcast_to(cand, (tq, kt_w))
            n_eq = count(lambda key, kb: (key == thr_b) & ((kb * kt_w + col) < cand_b))
            return jnp.where(n_eq <= need, cand, cut)

        cut = lax.fori_loop(0, n_cols.bit_length(), cut_step, jnp.zeros((tq, 1), I32))
        cut_ref[...] = jnp.broadcast_to(cut, cut_ref.shape)

    cut_b = jnp.broadcast_to(cut_ref[:, 0:1], (tq, kt_w))

    def mask_block(kb, _):
        start = kb * kt_w
        key = key_ref[kb]
        sel = (key > thr_b) | ((key == thr_b) & ((start + col) < cut_b))
        mb_ref[kb] = jnp.where(sel & admissible(start), 0.0, NEG)
        return 0

    lax.fori_loop(0, n_blocks, mask_block, 0)

    for pair in range(WIDTH // LANES):
        lanes = slice(pair * LANES, (pair + 1) * LANES)
        q_pair = q_ref[:, lanes]
        outs = []
        for hh in range(2):
            head = 2 * pair + hh
            q_head = jnp.where(half_masks[hh], q_pair, 0)

            def attend(kb, carry, head=head, q_head=q_head, lanes=lanes):
                m, l, acc = carry
                start = pl.multiple_of(kb * kt_w, kt_w)
                s = _dot_nt(q_head, k_ref[pl.ds(start, kt_w), lanes])
                bias = []
                for sub in range(kt_w // LANES):
                    d = jnp.clip((p0 - (start + sub * LANES)) // LANES, 0, N_BIAS_TILES - 1)
                    bias.append(tb_ref[head * N_BIAS_TILES + d, pl.ds(0, tq), :])
                s = s + jnp.concatenate(bias, axis=1) + mb_ref[kb]
                m_new = jnp.maximum(m, jnp.max(s, axis=1, keepdims=True))
                alpha = jnp.exp(m - m_new)
                p = jnp.exp(s - m_new)
                l = alpha * l + jnp.sum(p, axis=1, keepdims=True)
                acc = alpha * acc + _dot(p.astype(MXU_DTYPE), v_ref[pl.ds(start, kt_w), lanes])
                return m_new, l, acc

            init = (jnp.full((tq, 1), NEG, F32), jnp.zeros((tq, 1), F32), jnp.zeros((tq, LANES), F32))
            _, l, acc = lax.fori_loop(0, n_blocks, attend, init)
            outs.append(acc / l)
        o_ref[:, lanes] = jnp.where(half_masks[0], outs[0], outs[1]).astype(o_ref.dtype)


def _sparse_attention(q, k, v, qi, kid, wi, tiles, q_off, n_keys):
    b, t, _ = q.shape
    lp = k.shape[1]
    tq = min(Q_TILE, t)
    assert t % tq == 0 and lp % KEY_TILE == 0 and q_off % LANES == 0 and (tq == LANES or t == tq)
    top_k = min(TOPK_MAX, n_keys // 4)
    n_blocks = lp // KEY_TILE
    q_spec = lambda w: pl.BlockSpec((None, tq, w), lambda bi, qi_: (bi, qi_, 0))
    k_spec = lambda w: pl.BlockSpec((None, lp, w), lambda bi, qi_: (bi, 0, 0))
    return pl.pallas_call(
        functools.partial(_sparse_kernel, q_off=q_off, tq=tq, n_keys=n_keys, top_k=top_k),
        grid=(b, t // tq),
        in_specs=[q_spec(WIDTH), k_spec(WIDTH), k_spec(WIDTH), q_spec(WIDTH), k_spec(LANES), q_spec(LANES),
                  pl.BlockSpec(tiles.shape, lambda bi, qi_: (0, 0, 0))],
        out_specs=q_spec(WIDTH),
        out_shape=jax.ShapeDtypeStruct(q.shape, MXU_DTYPE),
        scratch_shapes=[pltpu.VMEM((n_blocks, tq, KEY_TILE), I32),
                        pltpu.VMEM((n_blocks, tq, KEY_TILE), F32),
                        pltpu.VMEM((tq, LANES), I32)],
        compiler_params=_params(("parallel", "arbitrary")),
        name="sparse_attention",
    )(q, k, v, qi, kid, wi, tiles)


def _ffn_kernel(x_ref, oa_ref, ob_ref, wo_ref, g2_ref, wup_ref, cw_ref, cb_ref, wdn_ref, st_ref,
                y_ref, cs_ref, x2_ref, h2_ref, acc_ref, prev_ref, *, tm, n_chunks):
    keep = CONV_WIDTH - 1

    @pl.when(pl.program_id(1) == 0)
    def _():
        prev_ref[:, :, SUBLANES - keep:, :] = st_ref[...]

    mix = jnp.concatenate([oa_ref[...], ob_ref[...]], axis=1)
    x2 = x_ref[...] + _dot(mix, wo_ref[...])
    x2_ref[...] = x2
    ms = jnp.mean(x2 * x2, axis=-1, keepdims=True)
    h2_ref[...] = (x2 * lax.rsqrt(ms + EPS) * g2_ref[...]).astype(MXU_DTYPE)
    acc_ref[...] = jnp.zeros(acc_ref.shape, F32)
    row = lax.broadcasted_iota(I32, (tm, 1), 0)

    def chunk(c, _):
        h2 = h2_ref[...]
        conv = []
        for half in range(2):
            u = _dot(h2, wup_ref[half, c])
            prev = prev_ref[half, c]
            before2, before1 = prev[SUBLANES - 2:SUBLANES - 1, :], prev[SUBLANES - 1:, :]
            u1 = jnp.where(row == 0, before1, pltpu.roll(u, 1, 0))
            u2 = jnp.where(row == 0, before2, jnp.where(row == 1, before1, pltpu.roll(u, 2, 0)))
            w = cw_ref[half, c]
            conv.append(cb_ref[half, c] + u2 * w[0:1, :] + u1 * w[1:2, :] + u * w[2:3, :])
            prev_ref[half, c] = u[tm - SUBLANES:, :]
            cs_ref[half, c] = u[tm - keep:, :]
        a, g = conv
        act = g * (1.0 / (1.0 + jnp.exp(-g))) * a
        acc_ref[...] += _dot(act.astype(MXU_DTYPE), wdn_ref[c])
        return 0

    lax.fori_loop(0, n_chunks, chunk, 0)
    y_ref[...] = x2_ref[...] + acc_ref[...]


def _ffn(x, oa, ob, w_out, g2, w_up, conv_w, conv_b, w_down, state):
    b, t, d = x.shape
    tm = min(ROW_TILE, t)
    assert t % tm == 0 and tm >= SUBLANES and CONV_WIDTH == 3
    _, n_chunks, _, fc = w_up.shape
    full = lambda a: pl.BlockSpec(a.shape, lambda bi, ti: (0,) * a.ndim)
    row_spec = lambda w: pl.BlockSpec((None, tm, w), lambda bi, ti: (bi, ti, 0))
    st_spec = pl.BlockSpec((None,) + state.shape[1:], lambda bi, ti: (bi, 0, 0, 0, 0))
    return pl.pallas_call(
        functools.partial(_ffn_kernel, tm=tm, n_chunks=n_chunks),
        grid=(b, t // tm),
        in_specs=[row_spec(d), row_spec(WIDTH), row_spec(WIDTH), full(w_out), full(g2), full(w_up),
                  full(conv_w), full(conv_b), full(w_down), st_spec],
        out_specs=[row_spec(d), st_spec],
        out_shape=[jax.ShapeDtypeStruct(x.shape, F32), jax.ShapeDtypeStruct(state.shape, F32)],
        scratch_shapes=[pltpu.VMEM((tm, d), F32), pltpu.VMEM((tm, d), MXU_DTYPE), pltpu.VMEM((tm, d), F32),
                        pltpu.VMEM((2, n_chunks, SUBLANES, fc), F32)],
        compiler_params=_params(("parallel", "arbitrary")),
        name="ffn",
    )(x, oa, ob, w_out, g2, w_up, conv_w, conv_b, w_down, state)


def _pad_keys(a, lp):
    return jnp.pad(a, ((0, 0), (0, lp - a.shape[1]), (0, 0)))


def _state_to_chunks(s, fc):
    b, keep, f2 = s.shape
    return s.reshape(b, keep, 2, f2 // 2 // fc, fc).transpose(0, 2, 3, 1, 4)


def _state_from_chunks(s):
    b, _, n_chunks, keep, fc = s.shape
    return s.transpose(0, 3, 1, 2, 4).reshape(b, keep, 2 * n_chunks * fc)


def _layer(x, caches, lw, consts):
    b, t, d = x.shape
    tri, seg, tiles = consts
    outs = _in_proj(x.reshape(b * t, d), lw["g1"], lw["w_main"], lw["w_tail"], lw["gq"], lw["gk"], seg)
    ak, av, bk, bv, ik = [o.reshape(b, t, -1) for o in outs[:5]]
    qa, ka, va, qb, kb, vb, qi, kid, wi = [o.reshape(b, t, -1) for o in outs[5:]]
    fc = lw["w_up"].shape[-1]
    if caches is None:
        past = 0
        state = jnp.zeros((b, 2, lw["w_up"].shape[1], CONV_WIDTH - 1, fc), F32)
    else:
        c_ak, c_av, c_bk, c_bv, c_ik, c_conv = caches
        past = c_ak.shape[1]
        flat = lambda c: c.reshape(b, past, -1).astype(MXU_DTYPE)
        ka = jnp.concatenate([flat(c_ak), ka], axis=1)
        va = jnp.concatenate([flat(c_av), va], axis=1)
        kb = jnp.concatenate([flat(c_bk), kb], axis=1)
        vb = jnp.concatenate([flat(c_bv), vb], axis=1)
        kid = jnp.concatenate([jnp.concatenate([flat(c_ik)] * 2, axis=-1), kid], axis=1)
        state = _state_to_chunks(c_conv, fc)
    n_keys = past + t
    lp = -(-n_keys // KEY_TILE) * KEY_TILE
    ka, va, kb, vb, kid = [_pad_keys(a, lp) for a in (ka, va, kb, vb, kid)]
    oa = _stick_breaking(qa, ka, va, tri, past)
    ob = _sparse_attention(qb, kb, vb, qi, kid, wi, tiles, past, n_keys)
    y, new_state = _ffn(x, oa, ob, lw["w_out"], lw["g2"], lw["w_up"], lw["conv_w"], lw["conv_b"],
                        lw["w_down"], state)
    heads = lambda a: a.reshape(b, t, N_HEADS, HEAD_DIM)
    return y, (heads(ak), heads(av), heads(bk), heads(bv), ik, _state_from_chunks(new_state))


def _layer_weights(l, norm1, w_in, q_norm, k_norm, w_out, norm2, w_up, conv_w, conv_b, w_down):
    d = w_in.shape[1]
    f = w_down.shape[1]
    fc = FF_TILE if f % FF_TILE == 0 else LANES
    assert f % fc == 0 and w_in.shape[2] == 7 * WIDTH + IDX_DIM + N_HEADS
    n_chunks = f // fc
    w = w_in[l]
    main = 7 * WIDTH
    w_ik, w_iw = w[:, main:main + IDX_DIM], w[:, main + IDX_DIM:]
    w_tail = jnp.concatenate([w_ik, w_ik, w_iw, jnp.zeros((d, LANES - N_HEADS), w.dtype)], axis=1)
    return {
        "g1": norm1[l][None, :],
        "w_main": w[:, :main].astype(MXU_DTYPE),
        "w_tail": w_tail.astype(MXU_DTYPE),
        "gq": jnp.tile(q_norm[l], N_HEADS)[None, :],
        "gk": jnp.tile(k_norm[l], N_HEADS)[None, :],
        "w_out": w_out[l].astype(MXU_DTYPE),
        "g2": norm2[l][None, :],
        "w_up": w_up[l].astype(MXU_DTYPE).reshape(d, 2, n_chunks, fc).transpose(1, 2, 0, 3),
        "conv_w": conv_w[l].reshape(CONV_WIDTH, 2, n_chunks, fc).transpose(1, 2, 0, 3),
        "conv_b": conv_b[l].reshape(2, n_chunks, 1, fc),
        "w_down": w_down[l].astype(MXU_DTYPE).reshape(n_chunks, fc, d),
    }


def kernel(x_prompt, x_sample, cache_a_k, cache_a_v, cache_b_k, cache_b_v, cache_idx_k, state_ffn_conv,
           rel_bias, norm1, w_in, q_norm, k_norm, w_out, norm2, w_up, conv_w, conv_b, w_down):
    depth = w_in.shape[0]
    assert cache_a_k.shape[3:] == (N_HEADS, HEAD_DIM) and cache_idx_k.shape[-1] == IDX_DIM
    r = jnp.arange(LANES, dtype=I32)
    tri = (r[:, None] > r[None, :]).astype(MXU_DTYPE)
    tri = jnp.concatenate([tri, tri], axis=0)
    g = jnp.arange(WIDTH, dtype=I32) // HEAD_DIM
    seg = (g[:, None] == g[None, :]).astype(MXU_DTYPE)
    seg = jnp.concatenate([seg, seg], axis=0)
    consts = (tri, seg, _bias_tiles(rel_bias))

    yp, ys = x_prompt, x_sample
    p_states, s_states = [], []
    for l in range(depth):
        lw = _layer_weights(l, norm1, w_in, q_norm, k_norm, w_out, norm2, w_up, conv_w, conv_b, w_down)
        yp, st_p = _layer(yp, None, lw, consts)
        caches = (cache_a_k[l], cache_a_v[l], cache_b_k[l], cache_b_v[l], cache_idx_k[l], state_ffn_conv[l])
        ys, st_s = _layer(ys, caches, lw, consts)
        p_states.append(st_p)
        s_states.append(st_s)
    stack = lambda sts, i: jnp.stack([s[i] for s in sts], axis=0)
    return (yp, ys) + tuple(stack(p_states, i) for i in range(6)) + tuple(stack(s_states, i) for i in range(6))
```

```python
import functools
import math

import jax
import jax.numpy as jnp
from jax import lax
from jax.experimental import pallas as pl
from jax.experimental.pallas import tpu as pltpu

HEAD_DIM = 64
N_HEADS = 8
IDX_DIM = 64
WIDTH = N_HEADS * HEAD_DIM
CHUNK = 64
CHUNK_SHIFT = 6
TOPK_MAX = 256
N_BUCKETS = 32
T5_MAX_DISTANCE = 128
CONV_WIDTH = 3
EPS = 1e-6

LANES = 128
SUBLANES = 8
Q_TILE = 128
SB_Q_TILE = 256
SB_KEY_TILE = 256
KEY_TILE = 256
ROW_TILE = 256
FF_TILE = 256
VMEM_LIMIT_BYTES = 56 * 1024 * 1024

MXU_DTYPE = jnp.bfloat16
F32 = jnp.float32
I32 = jnp.int32
INT_MIN = -(2 ** 31)
NEG = -1e30

_NT = (((1,), (1,)), ((), ()))


def _dot(a, b):
    return jnp.dot(a, b, preferred_element_type=F32)


def _dot_nt(a, b):
    return lax.dot_general(a, b, _NT, preferred_element_type=F32)


def _split_hi_lo(x):
    hi = x.astype(MXU_DTYPE)
    lo = (x - hi.astype(F32)).astype(MXU_DTYPE)
    return jnp.concatenate([hi, lo], axis=1)


def _params(semantics):
    return pltpu.CompilerParams(dimension_semantics=semantics, vmem_limit_bytes=VMEM_LIMIT_BYTES)


def _in_proj_kernel(x_ref, g1_ref, w_ref, wt_ref, gq_ref, gk_ref, seg_ref,
                    ak_ref, av_ref, bk_ref, bv_ref, ik_ref,
                    qa_ref, ka_ref, va_ref, qb_ref, kb_ref, vb_ref, qi_ref, kid_ref, wi_ref):
    x = x_ref[...]
    ms = jnp.mean(x * x, axis=-1, keepdims=True)
    h = (x * lax.rsqrt(ms + EPS) * g1_ref[...]).astype(MXU_DTYPE)

    def proj(j):
        return _dot(h, w_ref[:, j * WIDTH:(j + 1) * WIDTH])

    def head_norm(p, g_ref):
        ss = _dot(_split_hi_lo(p * p), seg_ref[...])
        return p * lax.rsqrt(ss * (1.0 / HEAD_DIM) + EPS) * g_ref[...]

    scale = HEAD_DIM ** -0.5
    qa_ref[...] = (proj(0) * scale).astype(MXU_DTYPE)
    p = proj(1)
    ak_ref[...] = p
    ka_ref[...] = p.astype(MXU_DTYPE)
    p = proj(2)
    av_ref[...] = p
    va_ref[...] = p.astype(MXU_DTYPE)
    qb_ref[...] = (head_norm(proj(3), gq_ref) * scale).astype(MXU_DTYPE)
    p = head_norm(proj(4), gk_ref)
    bk_ref[...] = p
    kb_ref[...] = p.astype(MXU_DTYPE)
    p = proj(5)
    bv_ref[...] = p
    vb_ref[...] = p.astype(MXU_DTYPE)
    qi_ref[...] = proj(6).astype(MXU_DTYPE)
    t = _dot(h, wt_ref[...])
    ik_ref[...] = t[:, :IDX_DIM]
    kid_ref[...] = t[:, :LANES].astype(MXU_DTYPE)
    wi_ref[...] = t[:, LANES:] * (IDX_DIM ** -0.5 * N_HEADS ** -0.5)


def _in_proj(x2d, g1, w_main, w_tail, gq, gk, seg):
    rows, d = x2d.shape
    tm = min(ROW_TILE, rows)
    assert rows % tm == 0
    row_spec = lambda w: pl.BlockSpec((tm, w), lambda i: (i, 0))
    full = lambda a: pl.BlockSpec(a.shape, lambda i: (0,) * a.ndim)
    f32_out = lambda w: jax.ShapeDtypeStruct((rows, w), F32)
    mx_out = lambda w: jax.ShapeDtypeStruct((rows, w), MXU_DTYPE)
    return pl.pallas_call(
        _in_proj_kernel,
        grid=(rows // tm,),
        in_specs=[row_spec(d), full(g1), full(w_main), full(w_tail), full(gq), full(gk), full(seg)],
        out_specs=[row_spec(WIDTH)] * 4 + [row_spec(IDX_DIM)] + [row_spec(WIDTH)] * 7
                  + [row_spec(LANES), row_spec(LANES)],
        out_shape=[f32_out(WIDTH)] * 4 + [f32_out(IDX_DIM)] + [mx_out(WIDTH)] * 7
                  + [mx_out(LANES), f32_out(LANES)],
        compiler_params=_params(("parallel",)),
        name="in_proj",
    )(x2d, g1, w_main, w_tail, gq, gk, seg)


def _head_block_diag(x, first_half):
    return jnp.concatenate([jnp.where(first_half, x, 0), jnp.where(first_half, 0, x)], axis=0)


def _stick_breaking_kernel(q_ref, k_ref, v_ref, tri_ref, o_ref, lb_ref, hi_ref, lo_ref, w_ref, *, q_off, tq):
    p0 = q_off + pl.program_id(2) * tq
    q = q_ref[...]
    lane = lax.broadcasted_iota(I32, (1, LANES), 1)
    lane2 = lax.broadcasted_iota(I32, (1, 2 * LANES), 1)
    first_half = lane < HEAD_DIM
    key_off = lane2 & (LANES - 1)
    row_pos = p0 + lax.broadcasted_iota(I32, (tq, 1), 0)
    n_blocks = (p0 + tq - 1 + LANES - 1) // LANES
    n_diag = n_blocks - p0 // LANES
    tri = tri_ref[...]

    def block_start(j):
        return pl.multiple_of(jnp.clip(n_blocks - 1 - j, 0, n_blocks - 1) * LANES, LANES)

    def half_step(masked, j, slot, carry):
        acc, later0, later1, later0_p, later1_p = carry
        other = 1 - slot
        start = block_start(j)
        k_bd = _head_block_diag(k_ref[pl.ds(start, LANES), :], first_half)
        z = _dot_nt(q, k_bd)
        log_beta = jnp.minimum(z, 0.0) - jnp.log(1.0 + jnp.exp(-jnp.abs(z)))
        log_1mb = log_beta - z
        if masked:
            key_pos = jnp.where(j < n_blocks, start, q_off + tq * pl.num_programs(2)) + key_off
            causal = key_pos < row_pos
            log_1mb = jnp.where(causal, log_1mb, 0.0)
            log_beta = jnp.where(causal, log_beta, NEG)
        hi = log_1mb.astype(MXU_DTYPE)
        lb_ref[slot] = log_beta
        hi_ref[slot] = hi
        lo_ref[slot] = (log_1mb - hi.astype(F32)).astype(MXU_DTYPE)
        new_later0 = later0 + jnp.sum(log_1mb[:, :LANES], axis=1, keepdims=True)
        new_later1 = later1 + jnp.sum(log_1mb[:, LANES:], axis=1, keepdims=True)
        within = _dot(hi_ref[other], tri) + _dot(lo_ref[other], tri)
        later = jnp.concatenate([jnp.broadcast_to(later0_p, (tq, LANES)),
                                 jnp.broadcast_to(later1_p, (tq, LANES))], axis=1)
        w_pp = w_ref[slot]
        w_ref[other] = jnp.exp(lb_ref[other] + (within + later)).astype(MXU_DTYPE)
        v_bd = _head_block_diag(v_ref[pl.ds(block_start(j - 2), LANES), :], first_half)
        acc = acc + _dot(w_pp, v_bd)
        return acc, new_later0, new_later1, later0, later1

    def pair_step(masked, p, carry):
        carry = half_step(masked, 2 * p, 0, carry)
        return half_step(masked, 2 * p + 1, 1, carry)

    lb_ref[...] = jnp.full(lb_ref.shape, NEG, F32)
    hi_ref[...] = jnp.zeros(hi_ref.shape, MXU_DTYPE)
    lo_ref[...] = jnp.zeros(lo_ref.shape, MXU_DTYPE)
    w_ref[...] = jnp.zeros(w_ref.shape, MXU_DTYPE)
    zero_col = row_pos.astype(F32) * 0.0
    carry = (jnp.zeros((tq, LANES), F32), zero_col, zero_col, zero_col, zero_col)
    pairs_diag = (n_diag + 1) // 2
    pairs_full = jnp.maximum(n_blocks // 2, pairs_diag)
    pairs_all = (n_blocks + 2 + 1) // 2
    carry = lax.fori_loop(0, pairs_diag, functools.partial(pair_step, True), carry)
    carry = lax.fori_loop(pairs_diag, pairs_full, functools.partial(pair_step, False), carry)
    carry = lax.fori_loop(pairs_full, pairs_all, functools.partial(pair_step, True), carry)
    o_ref[...] = carry[0].astype(o_ref.dtype)


def _stick_breaking(q, k, v, tri, q_off):
    b, t, _ = q.shape
    lp = k.shape[1]
    tq = min(SB_Q_TILE, t)
    assert t % tq == 0 and lp % SB_KEY_TILE == 0
    pairs = WIDTH // LANES
    kv_spec = pl.BlockSpec((None, lp, LANES), lambda bi, pi, qi: (bi, 0, pi))
    q_spec = pl.BlockSpec((None, tq, LANES), lambda bi, pi, qi: (bi, qi, pi))
    return pl.pallas_call(
        functools.partial(_stick_breaking_kernel, q_off=q_off, tq=tq),
        grid=(b, pairs, t // tq),
        in_specs=[q_spec, kv_spec, kv_spec, pl.BlockSpec(tri.shape, lambda bi, pi, qi: (0, 0))],
        out_specs=q_spec,
        out_shape=jax.ShapeDtypeStruct(q.shape, MXU_DTYPE),
        scratch_shapes=[pltpu.VMEM((2, tq, 2 * LANES), F32)] + [pltpu.VMEM((2, tq, 2 * LANES), MXU_DTYPE)] * 3,
        compiler_params=_params(("parallel", "parallel", "arbitrary")),
        name="stick_breaking",
    )(q, k, v, tri)


def _bias_tiles_kernel(rb_ref, bucket_ref, tb_ref):
    h = pl.program_id(0)
    for d in range(bucket_ref.shape[0]):
        bucket = bucket_ref[d]
        acc = jnp.zeros(bucket.shape, F32)
        for j in range(N_BUCKETS):
            acc = jnp.where(bucket == j, rb_ref[j, h], acc)
        tb_ref[d] = acc


def _t5_bucket(rel):
    half = N_BUCKETS // 2
    max_exact = half // 2
    base = jnp.where(rel > 0, half, 0)
    n = jnp.abs(rel)
    large = max_exact + (jnp.log(jnp.maximum(n, 1).astype(jnp.float32) / max_exact)
                         / math.log(T5_MAX_DISTANCE / max_exact) * (half - max_exact)).astype(jnp.int32)
    large = jnp.minimum(large, half - 1)
    return base + jnp.where(n < max_exact, n, large)


N_BIAS_TILES = 3


def _bias_tiles(rel_bias):
    assert T5_MAX_DISTANCE <= LANES
    j = jnp.arange(LANES, dtype=I32)[None, :, None]
    i = jnp.arange(LANES, dtype=I32)[None, None, :]
    d = jnp.arange(N_BIAS_TILES, dtype=I32)[:, None, None]
    bucket = _t5_bucket(j - i - LANES * d).astype(I32)
    return pl.pallas_call(
        _bias_tiles_kernel,
        grid=(N_HEADS,),
        in_specs=[pl.BlockSpec(memory_space=pltpu.SMEM),
                  pl.BlockSpec(bucket.shape, lambda h: (0, 0, 0))],
        out_specs=pl.BlockSpec((N_BIAS_TILES, LANES, LANES), lambda h: (h, 0, 0)),
        out_shape=jax.ShapeDtypeStruct((N_HEADS * N_BIAS_TILES, LANES, LANES), F32),
        compiler_params=_params(("arbitrary",)),
        name="bias_tiles",
    )(rel_bias, bucket)


def _sparse_kernel(q_ref, k_ref, vt_ref, qi_ref, kid_ref, wit_ref, tb_ref, ot_ref,
                   key_ref, mb_ref, cut_ref, *, q_off, tq, n_keys, top_k):
    kt_w = KEY_TILE
    p0 = q_off + pl.program_id(1) * tq
    lane = lax.broadcasted_iota(I32, (1, LANES), 1)
    key_row = lax.broadcasted_iota(I32, (kt_w, tq), 0)
    q_chunk = lax.shift_right_arithmetic(p0 + lax.broadcasted_iota(I32, (1, tq), 1), CHUNK_SHIFT)
    adm_end = (lax.shift_right_arithmetic(p0 + tq - 1, CHUNK_SHIFT) + 1) * CHUNK
    n_blocks = (jnp.minimum(adm_end, n_keys) + kt_w - 1) // kt_w
    half_masks = [lane < HEAD_DIM, lane >= HEAD_DIM]

    def admissible(start):
        pos = start + key_row
        return (lax.shift_right_arithmetic(pos, CHUNK_SHIFT) <= q_chunk) & (pos < n_keys)

    wit = wit_ref[...]
    qi_heads = []
    for h in range(N_HEADS):
        pair = qi_ref[:, (h // 2) * LANES:(h // 2 + 1) * LANES]
        qi_heads.append(jnp.where(half_masks[h % 2], pair, 0))

    def score_block(kb, _):
        start = pl.multiple_of(kb * kt_w, kt_w)
        kt = kid_ref[pl.ds(start, kt_w), :]
        acc = jnp.zeros((kt_w, tq), F32)
        for h in range(N_HEADS):
            acc = acc + jnp.maximum(_dot_nt(kt, qi_heads[h]), 0.0) * wit[h:h + 1, :]
        acc = jnp.where(acc == 0.0, 0.0, acc)
        bits = lax.bitcast_convert_type(acc, I32)
        key = jnp.where(bits < 0, bits ^ 0x7FFFFFFF, bits)
        key_ref[kb] = jnp.where(admissible(start), key, INT_MIN)
        return 0

    lax.fori_loop(0, n_blocks, score_block, 0)

    def count(pred):
        def blk(kb, cnt):
            hit = jnp.where(pred(key_ref[kb], kb), 1.0, 0.0)
            parts = [hit[r * SUBLANES:(r + 1) * SUBLANES, :] for r in range(kt_w // SUBLANES)]
            while len(parts) > 1:
                parts = [a + b for a, b in zip(parts[::2], parts[1::2])]
            return cnt + parts[0]
        cnt = lax.fori_loop(0, n_blocks, blk, jnp.zeros((SUBLANES, tq), F32))
        return jnp.sum(cnt, axis=0, keepdims=True)

    k_f = float(top_k)
    thr = jnp.where(count(lambda key, kb: key >= 0) >= k_f, 0, INT_MIN).astype(I32)

    def bit_step(i, thr):
        cand = thr | lax.shift_left(jnp.int32(1), 30 - i)
        return jnp.where(count(lambda key, kb: key >= cand) >= k_f, cand, thr)

    thr = lax.fori_loop(0, 31, bit_step, thr)

    n_ge = count(lambda key, kb: key >= thr)
    n_cols = key_ref.shape[0] * kt_w
    cut_ref[...] = jnp.full(cut_ref.shape, n_cols, I32)

    @pl.when(jnp.max(n_ge) > k_f)
    def _():
        need = k_f - count(lambda key, kb: key > thr)

        def cut_step(i, cut):
            cand = cut | lax.shift_left(jnp.int32(1), n_cols.bit_length() - 1 - i)
            n_eq = count(lambda key, kb: (key == thr) & ((kb * kt_w + key_row) < cand))
            return jnp.where(n_eq <= need, cand, cut)

        cut = lax.fori_loop(0, n_cols.bit_length(), cut_step, jnp.zeros((1, tq), I32))
        cut_ref[...] = jnp.broadcast_to(cut, cut_ref.shape)

    cut = cut_ref[0:1, :]

    def mask_block(kb, _):
        start = kb * kt_w
        key = key_ref[kb]
        sel = (key > thr) | ((key == thr) & ((start + key_row) < cut))
        mb_ref[kb] = jnp.where(sel & admissible(start), 0.0, NEG)
        return 0

    lax.fori_loop(0, n_blocks, mask_block, 0)

    n_pairs = WIDTH // LANES
    first_half = half_masks[0]
    top_rows = lax.broadcasted_iota(I32, (LANES, 1), 0) < HEAD_DIM

    def attend(kb, carry):
        start = pl.multiple_of(kb * kt_w, kt_w)
        mb = mb_ref[kb]
        tile_d = [jnp.clip((p0 - (start + sub * LANES)) // LANES, 0, N_BIAS_TILES - 1)
                  for sub in range(kt_w // LANES)]
        out = []
        for pair in range(n_pairs):
            lanes = slice(pair * LANES, (pair + 1) * LANES)
            m, l, acc = carry[pair]
            k_bd = _head_block_diag(k_ref[pl.ds(start, kt_w), lanes], first_half)
            vt = vt_ref[lanes, pl.ds(start, kt_w)]
            vt_bd = jnp.concatenate([jnp.where(top_rows, vt, 0), jnp.where(top_rows, 0, vt)], axis=1)
            z = _dot_nt(k_bd, q_ref[:, lanes])
            m_new, alpha, p = [], [], []
            for hh in range(2):
                head = 2 * pair + hh
                bias = [tb_ref[head * N_BIAS_TILES + d][:, :tq] for d in tile_d]
                s = z[hh * kt_w:(hh + 1) * kt_w, :] + jnp.concatenate(bias, axis=0) + mb
                m_h = jnp.maximum(m[hh], jnp.max(s, axis=0, keepdims=True))
                a_h = jnp.exp(m[hh] - m_h)
                p_h = jnp.exp(s - m_h)
                l = l[:hh] + (a_h * l[hh] + jnp.sum(p_h, axis=0, keepdims=True),) + l[hh + 1:]
                m_new.append(m_h)
                alpha.append(a_h)
                p.append(p_h.astype(MXU_DTYPE))
            acc = jnp.where(top_rows, alpha[0], alpha[1]) * acc + _dot(vt_bd, jnp.concatenate(p, axis=0))
            out.append((tuple(m_new), l, acc))
        return tuple(out)

    neg_row = jnp.full((1, tq), NEG, F32)
    zero_row = jnp.zeros((1, tq), F32)
    init = tuple(((neg_row, neg_row), (zero_row, zero_row), jnp.zeros((LANES, tq), F32)) for _ in range(n_pairs))
    final = lax.fori_loop(0, n_blocks, attend, init)
    for pair in range(n_pairs):
        _, l, acc = final[pair]
        ot_ref[pair * LANES:(pair + 1) * LANES, :] = (acc / jnp.where(top_rows, l[0], l[1])).astype(ot_ref.dtype)


def _sparse_attention(q, k, v, qi, kid, wi, tiles, q_off, n_keys):
    b, t, _ = q.shape
    lp = k.shape[1]
    tq = min(Q_TILE, t)
    assert t % tq == 0 and lp % KEY_TILE == 0 and q_off % LANES == 0 and (tq == LANES or t == tq)
    top_k = min(TOPK_MAX, n_keys // 4)
    n_blocks = lp // KEY_TILE
    vt = v.transpose(0, 2, 1)
    wit = wi[:, :, :N_HEADS].transpose(0, 2, 1)
    q_spec = lambda w: pl.BlockSpec((None, tq, w), lambda bi, qi_: (bi, qi_, 0))
    k_spec = lambda w: pl.BlockSpec((None, lp, w), lambda bi, qi_: (bi, 0, 0))
    qt_spec = lambda rows: pl.BlockSpec((None, rows, tq), lambda bi, qi_: (bi, 0, qi_))
    out_t = pl.pallas_call(
        functools.partial(_sparse_kernel, q_off=q_off, tq=tq, n_keys=n_keys, top_k=top_k),
        grid=(b, t // tq),
        in_specs=[q_spec(WIDTH), k_spec(WIDTH), pl.BlockSpec((None, WIDTH, lp), lambda bi, qi_: (bi, 0, 0)),
                  q_spec(WIDTH), k_spec(LANES), qt_spec(N_HEADS),
                  pl.BlockSpec(tiles.shape, lambda bi, qi_: (0, 0, 0))],
        out_specs=qt_spec(WIDTH),
        out_shape=jax.ShapeDtypeStruct((b, WIDTH, t), MXU_DTYPE),
        scratch_shapes=[pltpu.VMEM((n_blocks, KEY_TILE, tq), I32),
                        pltpu.VMEM((n_blocks, KEY_TILE, tq), F32),
                        pltpu.VMEM((SUBLANES, tq), I32)],
        compiler_params=_params(("parallel", "arbitrary")),
        name="sparse_attention",
    )(q, k, vt, qi, kid, wit, tiles)
    return out_t.transpose(0, 2, 1)


def _ffn_kernel(x_ref, oa_ref, ob_ref, wo_ref, g2_ref, wup_ref, cw_ref, cb_ref, wdn_ref, st_ref,
                y_ref, cs_ref, x2_ref, h2_ref, acc_ref, prev_ref, *, tm, n_chunks):
    keep = CONV_WIDTH - 1

    @pl.when(pl.program_id(1) == 0)
    def _():
        prev_ref[:, :, SUBLANES - keep:, :] = st_ref[...]

    mix = jnp.concatenate([oa_ref[...], ob_ref[...]], axis=1)
    x2 = x_ref[...] + _dot(mix, wo_ref[...])
    x2_ref[...] = x2
    ms = jnp.mean(x2 * x2, axis=-1, keepdims=True)
    h2_ref[...] = (x2 * lax.rsqrt(ms + EPS) * g2_ref[...]).astype(MXU_DTYPE)
    acc_ref[...] = jnp.zeros(acc_ref.shape, F32)
    row = lax.broadcasted_iota(I32, (tm, 1), 0)

    def chunk(c, _):
        h2 = h2_ref[...]
        conv = []
        for half in range(2):
            u = _dot(h2, wup_ref[half, c])
            prev = prev_ref[half, c]
            before2, before1 = prev[SUBLANES - 2:SUBLANES - 1, :], prev[SUBLANES - 1:, :]
            u1 = jnp.where(row == 0, before1, pltpu.roll(u, 1, 0))
            u2 = jnp.where(row == 0, before2, jnp.where(row == 1, before1, pltpu.roll(u, 2, 0)))
            w = cw_ref[half, c]
            conv.append(cb_ref[half, c] + u2 * w[0:1, :] + u1 * w[1:2, :] + u * w[2:3, :])
            prev_ref[half, c] = u[tm - SUBLANES:, :]
            cs_ref[half, c] = u[tm - keep:, :]
        a, g = conv
        act = g * (1.0 / (1.0 + jnp.exp(-g))) * a
        acc_ref[...] += _dot(act.astype(MXU_DTYPE), wdn_ref[c])
        return 0

    lax.fori_loop(0, n_chunks, chunk, 0)
    y_ref[...] = x2_ref[...] + acc_ref[...]


def _ffn(x, oa, ob, w_out, g2, w_up, conv_w, conv_b, w_down, state):
    b, t, d = x.shape
    tm = min(ROW_TILE, t)
    assert t % tm == 0 and tm >= SUBLANES and CONV_WIDTH == 3
    _, n_chunks, _, fc = w_up.shape
    full = lambda a: pl.BlockSpec(a.shape, lambda bi, ti: (0,) * a.ndim)
    row_spec = lambda w: pl.BlockSpec((None, tm, w), lambda bi, ti: (bi, ti, 0))
    st_spec = pl.BlockSpec((None,) + state.shape[1:], lambda bi, ti: (bi, 0, 0, 0, 0))
    return pl.pallas_call(
        functools.partial(_ffn_kernel, tm=tm, n_chunks=n_chunks),
        grid=(b, t // tm),
        in_specs=[row_spec(d), row_spec(WIDTH), row_spec(WIDTH), full(w_out), full(g2), full(w_up),
                  full(conv_w), full(conv_b), full(w_down), st_spec],
        out_specs=[row_spec(d), st_spec],
        out_shape=[jax.ShapeDtypeStruct(x.shape, F32), jax.ShapeDtypeStruct(state.shape, F32)],
        scratch_shapes=[pltpu.VMEM((tm, d), F32), pltpu.VMEM((tm, d), MXU_DTYPE), pltpu.VMEM((tm, d), F32),
                        pltpu.VMEM((2, n_chunks, SUBLANES, fc), F32)],
        compiler_params=_params(("parallel", "arbitrary")),
        name="ffn",
    )(x, oa, ob, w_out, g2, w_up, conv_w, conv_b, w_down, state)


def _pad_keys(a, lp):
    return jnp.pad(a, ((0, 0), (0, lp - a.shape[1]), (0, 0)))


def _state_to_chunks(s, fc):
    b, keep, f2 = s.shape
    return s.reshape(b, keep, 2, f2 // 2 // fc, fc).transpose(0, 2, 3, 1, 4)


def _state_from_chunks(s):
    b, _, n_chunks, keep, fc = s.shape
    return s.transpose(0, 3, 1, 2, 4).reshape(b, keep, 2 * n_chunks * fc)


def _layer(x, caches, lw, consts):
    b, t, d = x.shape
    tri, seg, tiles = consts
    outs = _in_proj(x.reshape(b * t, d), lw["g1"], lw["w_main"], lw["w_tail"], lw["gq"], lw["gk"], seg)
    ak, av, bk, bv, ik = [o.reshape(b, t, -1) for o in outs[:5]]
    qa, ka, va, qb, kb, vb, qi, kid, wi = [o.reshape(b, t, -1) for o in outs[5:]]
    fc = lw["w_up"].shape[-1]
    if caches is None:
        past = 0
        state = jnp.zeros((b, 2, lw["w_up"].shape[1], CONV_WIDTH - 1, fc), F32)
    else:
        c_ak, c_av, c_bk, c_bv, c_ik, c_conv = caches
        past = c_ak.shape[1]
        flat = lambda c: c.reshape(b, past, -1).astype(MXU_DTYPE)
        ka = jnp.concatenate([flat(c_ak), ka], axis=1)
        va = jnp.concatenate([flat(c_av), va], axis=1)
        kb = jnp.concatenate([flat(c_bk), kb], axis=1)
        vb = jnp.concatenate([flat(c_bv), vb], axis=1)
        kid = jnp.concatenate([jnp.concatenate([flat(c_ik)] * 2, axis=-1), kid], axis=1)
        state = _state_to_chunks(c_conv, fc)
    n_keys = past + t
    lp = -(-n_keys // KEY_TILE) * KEY_TILE
    ka, va, kb, vb, kid = [_pad_keys(a, lp) for a in (ka, va, kb, vb, kid)]
    oa = _stick_breaking(qa, ka, va, tri, past)
    ob = _sparse_attention(qb, kb, vb, qi, kid, wi, tiles, past, n_keys)
    y, new_state = _ffn(x, oa, ob, lw["w_out"], lw["g2"], lw["w_up"], lw["conv_w"], lw["conv_b"],
                        lw["w_down"], state)
    heads = lambda a: a.reshape(b, t, N_HEADS, HEAD_DIM)
    return y, (heads(ak), heads(av), heads(bk), heads(bv), ik, _state_from_chunks(new_state))


def _layer_weights(l, norm1, w_in, q_norm, k_norm, w_out, norm2, w_up, conv_w, conv_b, w_down):
    d = w_in.shape[1]
    f = w_down.shape[1]
    fc = FF_TILE if f % FF_TILE == 0 else LANES
    assert f % fc == 0 and w_in.shape[2] == 7 * WIDTH + IDX_DIM + N_HEADS
    n_chunks = f // fc
    w = w_in[l]
    main = 7 * WIDTH
    w_ik, w_iw = w[:, main:main + IDX_DIM], w[:, main + IDX_DIM:]
    w_tail = jnp.concatenate([w_ik, w_ik, w_iw, jnp.zeros((d, LANES - N_HEADS), w.dtype)], axis=1)
    return {
        "g1": norm1[l][None, :],
        "w_main": w[:, :main].astype(MXU_DTYPE),
        "w_tail": w_tail.astype(MXU_DTYPE),
        "gq": jnp.tile(q_norm[l], N_HEADS)[None, :],
        "gk": jnp.tile(k_norm[l], N_HEADS)[None, :],
        "w_out": w_out[l].astype(MXU_DTYPE),
        "g2": norm2[l][None, :],
        "w_up": w_up[l].astype(MXU_DTYPE).reshape(d, 2, n_chunks, fc).transpose(1, 2, 0, 3),
        "conv_w": conv_w[l].reshape(CONV_WIDTH, 2, n_chunks, fc).transpose(1, 2, 0, 3),
        "conv_b": conv_b[l].reshape(2, n_chunks, 1, fc),
        "w_down": w_down[l].astype(MXU_DTYPE).reshape(n_chunks, fc, d),
    }


def kernel(x_prompt, x_sample, cache_a_k, cache_a_v, cache_b_k, cache_b_v, cache_idx_k, state_ffn_conv,
           rel_bias, norm1, w_in, q_norm, k_norm, w_out, norm2, w_up, conv_w, conv_b, w_down):
    depth = w_in.shape[0]
    assert cache_a_k.shape[3:] == (N_HEADS, HEAD_DIM) and cache_idx_k.shape[-1] == IDX_DIM
    r2 = jnp.arange(2 * LANES, dtype=I32)
    tri = ((r2[:, None] > r2[None, :]) & (r2[:, None] // LANES == r2[None, :] // LANES)).astype(MXU_DTYPE)
    g = jnp.arange(WIDTH, dtype=I32) // HEAD_DIM
    seg = (g[:, None] == g[None, :]).astype(MXU_DTYPE)
    seg = jnp.concatenate([seg, seg], axis=0)
    consts = (tri, seg, _bias_tiles(rel_bias))

    yp, ys = x_prompt, x_sample
    p_states, s_states = [], []
    for l in range(depth):
        lw = _layer_weights(l, norm1, w_in, q_norm, k_norm, w_out, norm2, w_up, conv_w, conv_b, w_down)
        yp, st_p = _layer(yp, None, lw, consts)
        caches = (cache_a_k[l], cache_a_v[l], cache_b_k[l], cache_b_v[l], cache_idx_k[l], state_ffn_conv[l])
        ys, st_s = _layer(ys, caches, lw, consts)
        p_states.append(st_p)
        s_states.append(st_s)
    stack = lambda sts, i: jnp.stack([s[i] for s in sts], axis=0)
    return (yp, ys) + tuple(stack(p_states, i) for i in range(6)) + tuple(stack(s_states, i) for i in range(6))
```

```python
import functools
import math

import jax
import jax.numpy as jnp
from jax import lax
from jax.experimental import pallas as pl
from jax.experimental.pallas import tpu as pltpu

HEAD_DIM = 64
N_HEADS = 8
IDX_DIM = 64
WIDTH = N_HEADS * HEAD_DIM
CHUNK = 64
CHUNK_SHIFT = 6
TOPK_MAX = 256
N_BUCKETS = 32
T5_MAX_DISTANCE = 128
CONV_WIDTH = 3
EPS = 1e-6

LANES = 128
SUBLANES = 8
Q_TILE = 128
SB_Q_TILE = 256
SB_KEY_TILE = 256
SB_ROW_CHUNK = 32
SB_VANISH_LOG = -105.0
KEY_TILE = 512
ROW_TILE = 256
FF_TILE = 256
VMEM_LIMIT_BYTES = 56 * 1024 * 1024

MXU_DTYPE = jnp.bfloat16
F32 = jnp.float32
I32 = jnp.int32
INT_MIN = -(2 ** 31)
NEG = -1e30

_NT = (((1,), (1,)), ((), ()))


def _dot(a, b):
    return jnp.dot(a, b, preferred_element_type=F32)


def _dot_nt(a, b):
    return lax.dot_general(a, b, _NT, preferred_element_type=F32)


def _split_hi_lo(x):
    hi = x.astype(MXU_DTYPE)
    lo = (x - hi.astype(F32)).astype(MXU_DTYPE)
    return jnp.concatenate([hi, lo], axis=1)


def _params(semantics):
    return pltpu.CompilerParams(dimension_semantics=semantics, vmem_limit_bytes=VMEM_LIMIT_BYTES)


def _in_proj_kernel(x_ref, g1_ref, w_ref, wt_ref, gq_ref, gk_ref, seg_ref,
                    ak_ref, av_ref, bk_ref, bv_ref, ik_ref,
                    qa_ref, ka_ref, va_ref, qb_ref, kb_ref, vb_ref, qi_ref, kid_ref, wi_ref):
    x = x_ref[...]
    ms = jnp.mean(x * x, axis=-1, keepdims=True)
    h = (x * lax.rsqrt(ms + EPS) * g1_ref[...]).astype(MXU_DTYPE)

    def proj(j):
        return _dot(h, w_ref[:, j * WIDTH:(j + 1) * WIDTH])

    def head_norm(p, g_ref):
        ss = _dot(_split_hi_lo(p * p), seg_ref[...])
        return p * lax.rsqrt(ss * (1.0 / HEAD_DIM) + EPS) * g_ref[...]

    scale = HEAD_DIM ** -0.5
    qa_ref[...] = (proj(0) * scale).astype(MXU_DTYPE)
    p = proj(1)
    ak_ref[...] = p
    ka_ref[...] = p.astype(MXU_DTYPE)
    p = proj(2)
    av_ref[...] = p
    va_ref[...] = p.astype(MXU_DTYPE)
    qb_ref[...] = (head_norm(proj(3), gq_ref) * scale).astype(MXU_DTYPE)
    p = head_norm(proj(4), gk_ref)
    bk_ref[...] = p
    kb_ref[...] = p.astype(MXU_DTYPE)
    p = proj(5)
    bv_ref[...] = p
    vb_ref[...] = p.astype(MXU_DTYPE)
    qi_ref[...] = proj(6).astype(MXU_DTYPE)
    t = _dot(h, wt_ref[...])
    ik_ref[...] = t[:, :IDX_DIM]
    kid_ref[...] = t[:, :LANES].astype(MXU_DTYPE)
    wi_ref[...] = t[:, LANES:] * (IDX_DIM ** -0.5 * N_HEADS ** -0.5)


def _in_proj(x2d, g1, w_main, w_tail, gq, gk, seg):
    rows, d = x2d.shape
    tm = min(ROW_TILE, rows)
    assert rows % tm == 0
    row_spec = lambda w: pl.BlockSpec((tm, w), lambda i: (i, 0))
    full = lambda a: pl.BlockSpec(a.shape, lambda i: (0,) * a.ndim)
    f32_out = lambda w: jax.ShapeDtypeStruct((rows, w), F32)
    mx_out = lambda w: jax.ShapeDtypeStruct((rows, w), MXU_DTYPE)
    return pl.pallas_call(
        _in_proj_kernel,
        grid=(rows // tm,),
        in_specs=[row_spec(d), full(g1), full(w_main), full(w_tail), full(gq), full(gk), full(seg)],
        out_specs=[row_spec(WIDTH)] * 4 + [row_spec(IDX_DIM)] + [row_spec(WIDTH)] * 7
                  + [row_spec(LANES), row_spec(LANES)],
        out_shape=[f32_out(WIDTH)] * 4 + [f32_out(IDX_DIM)] + [mx_out(WIDTH)] * 7
                  + [mx_out(LANES), f32_out(LANES)],
        compiler_params=_params(("parallel",)),
        name="in_proj",
    )(x2d, g1, w_main, w_tail, gq, gk, seg)


def _head_block_diag(x, first_half):
    return jnp.concatenate([jnp.where(first_half, x, 0), jnp.where(first_half, 0, x)], axis=0)


def _stick_breaking_kernel(q_ref, k_ref, v_ref, tri_ref, o_ref, lb_ref, hl_ref, w_ref, *, q_off, tq):
    p0 = q_off + pl.program_id(2) * tq
    q = q_ref[...]
    lane = lax.broadcasted_iota(I32, (1, LANES), 1)
    lane2 = lax.broadcasted_iota(I32, (1, 2 * LANES), 1)
    first_half = lane < HEAD_DIM
    key_off = lane2 & (LANES - 1)
    row_pos = p0 + lax.broadcasted_iota(I32, (tq, 1), 0)
    n_blocks = (p0 + tq - 1 + LANES - 1) // LANES
    n_diag = n_blocks - p0 // LANES
    tri = tri_ref[...]

    def block_start(j):
        return pl.multiple_of(jnp.clip(n_blocks - 1 - j, 0, n_blocks - 1) * LANES, LANES)

    row_chunks = [slice(r, r + min(tq, SB_ROW_CHUNK)) for r in range(0, tq, min(tq, SB_ROW_CHUNK))]
    one = jnp.minimum(n_blocks, 1).astype(F32)

    def half_step(masked, j, slot, carry):
        acc, later0, later1, later0_p, later1_p = carry
        other = 1 - slot
        start = block_start(j)
        k_bd = _head_block_diag(k_ref[pl.ds(start, LANES), :], first_half)
        z = _dot_nt(q, k_bd)
        if masked:
            key_pos = jnp.where(j < n_blocks, start, q_off + tq * pl.num_programs(2)) + key_off
        sums = ([], [])
        for rows in row_chunks:
            zc = z[rows, :]
            log_beta = jnp.minimum(zc, 0.0) - jnp.log(one + jnp.exp(-jnp.abs(zc)))
            log_1mb = log_beta - zc
            if masked:
                causal = key_pos < row_pos[rows, :]
                log_1mb = jnp.where(causal, log_1mb, 0.0)
                log_beta = jnp.where(causal, log_beta, NEG)
            hi = log_1mb.astype(MXU_DTYPE)
            lb_ref[slot, rows, :] = log_beta
            hl_ref[slot, rows, :2 * LANES] = hi
            hl_ref[slot, rows, 2 * LANES:] = (log_1mb - hi.astype(F32)).astype(MXU_DTYPE)
            sums[0].append(jnp.sum(log_1mb[:, :LANES], axis=1, keepdims=True))
            sums[1].append(jnp.sum(log_1mb[:, LANES:], axis=1, keepdims=True))
        new_later0 = later0 + jnp.concatenate(sums[0], axis=0)
        new_later1 = later1 + jnp.concatenate(sums[1], axis=0)
        w_pp = w_ref[slot]
        within = _dot(hl_ref[other], tri)
        for rows in row_chunks:
            n = rows.stop - rows.start
            later = jnp.concatenate([jnp.broadcast_to(later0_p[rows, :], (n, LANES)),
                                     jnp.broadcast_to(later1_p[rows, :], (n, LANES))], axis=1)
            w_ref[other, rows, :] = jnp.exp(lb_ref[other, rows, :] + (within[rows, :] + later)).astype(MXU_DTYPE)
        v_bd = _head_block_diag(v_ref[pl.ds(block_start(j - 2), LANES), :], first_half)
        acc = acc + _dot(w_pp, v_bd)
        return acc, new_later0, new_later1, later0, later1

    def pair_step(masked, p, carry):
        carry = half_step(masked, 2 * p, 0, carry)
        return half_step(masked, 2 * p + 1, 1, carry)

    lb_ref[...] = jnp.full(lb_ref.shape, NEG, F32)
    hl_ref[...] = jnp.zeros(hl_ref.shape, MXU_DTYPE)
    w_ref[...] = jnp.zeros(w_ref.shape, MXU_DTYPE)
    zero_col = row_pos.astype(F32) * 0.0
    carry = (jnp.zeros((tq, LANES), F32), zero_col, zero_col, zero_col, zero_col)
    pairs_diag = (n_diag + 1) // 2
    pairs_full = jnp.maximum(n_blocks // 2, pairs_diag)
    pairs_all = (n_blocks + 2 + 1) // 2
    def weights_vanished(carry):
        return (jnp.maximum(jnp.max(carry[1]), jnp.max(carry[2])) < SB_VANISH_LOG).astype(I32)

    def unmasked_pair(state):
        p, _, carry = state
        carry = pair_step(False, p, carry)
        return p + 1, weights_vanished(carry), carry

    carry = lax.fori_loop(0, pairs_diag, functools.partial(pair_step, True), carry)
    p, vanished, carry = lax.while_loop(lambda s: (s[0] < pairs_full) & (s[1] == 0), unmasked_pair,
                                        (pairs_diag, weights_vanished(carry), carry))
    carry = lax.fori_loop(p, jnp.where(vanished == 1, p + 1, pairs_all), functools.partial(pair_step, True), carry)
    o_ref[...] = carry[0].astype(o_ref.dtype)


def _stick_breaking(q, k, v, tri, q_off):
    b, t, _ = q.shape
    lp = k.shape[1]
    tq = min(SB_Q_TILE, t)
    assert t % tq == 0 and lp % SB_KEY_TILE == 0
    pairs = WIDTH // LANES
    kv_spec = pl.BlockSpec((None, lp, LANES), lambda bi, pi, qi: (bi, 0, pi))
    q_spec = pl.BlockSpec((None, tq, LANES), lambda bi, pi, qi: (bi, qi, pi))
    return pl.pallas_call(
        functools.partial(_stick_breaking_kernel, q_off=q_off, tq=tq),
        grid=(b, pairs, t // tq),
        in_specs=[q_spec, kv_spec, kv_spec, pl.BlockSpec(tri.shape, lambda bi, pi, qi: (0, 0))],
        out_specs=q_spec,
        out_shape=jax.ShapeDtypeStruct(q.shape, MXU_DTYPE),
        scratch_shapes=[pltpu.VMEM((2, tq, 2 * LANES), F32),
                        pltpu.VMEM((2, tq, 4 * LANES), MXU_DTYPE),
                        pltpu.VMEM((2, tq, 2 * LANES), MXU_DTYPE)],
        compiler_params=_params(("parallel", "parallel", "arbitrary")),
        name="stick_breaking",
    )(q, k, v, tri)


def _bias_tiles_kernel(rb_ref, bucket_ref, tb_ref):
    h = pl.program_id(0)
    for d in range(bucket_ref.shape[0]):
        bucket = bucket_ref[d]
        acc = jnp.zeros(bucket.shape, F32)
        for j in range(N_BUCKETS):
            acc = jnp.where(bucket == j, rb_ref[j, h], acc)
        tb_ref[d] = acc


def _t5_bucket(rel):
    half = N_BUCKETS // 2
    max_exact = half // 2
    base = jnp.where(rel > 0, half, 0)
    n = jnp.abs(rel)
    large = max_exact + (jnp.log(jnp.maximum(n, 1).astype(jnp.float32) / max_exact)
                         / math.log(T5_MAX_DISTANCE / max_exact) * (half - max_exact)).astype(jnp.int32)
    large = jnp.minimum(large, half - 1)
    return base + jnp.where(n < max_exact, n, large)


N_BIAS_TILES = 3


def _bias_tiles(rel_bias):
    assert T5_MAX_DISTANCE <= LANES
    j = jnp.arange(LANES, dtype=I32)[None, :, None]
    i = jnp.arange(LANES, dtype=I32)[None, None, :]
    d = jnp.arange(N_BIAS_TILES, dtype=I32)[:, None, None]
    bucket = _t5_bucket(j - i - LANES * d).astype(I32)
    return pl.pallas_call(
        _bias_tiles_kernel,
        grid=(N_HEADS,),
        in_specs=[pl.BlockSpec(memory_space=pltpu.SMEM),
                  pl.BlockSpec(bucket.shape, lambda h: (0, 0, 0))],
        out_specs=pl.BlockSpec((N_BIAS_TILES, LANES, LANES), lambda h: (h, 0, 0)),
        out_shape=jax.ShapeDtypeStruct((N_HEADS * N_BIAS_TILES, LANES, LANES), F32),
        compiler_params=_params(("arbitrary",)),
        name="bias_tiles",
    )(rel_bias, bucket)


def _sparse_kernel(q_ref, k_ref, vt_ref, qi_ref, kid_ref, wit_ref, tb_ref, ot_ref,
                   key_ref, mb_ref, cut_ref, *, q_off, tq, n_keys, top_k):
    kt_w = KEY_TILE
    p0 = q_off + pl.program_id(1) * tq
    lane = lax.broadcasted_iota(I32, (1, LANES), 1)
    key_row = lax.broadcasted_iota(I32, (kt_w, tq), 0)
    q_chunk = lax.shift_right_arithmetic(p0 + lax.broadcasted_iota(I32, (1, tq), 1), CHUNK_SHIFT)
    adm_end = (lax.shift_right_arithmetic(p0 + tq - 1, CHUNK_SHIFT) + 1) * CHUNK
    n_blocks = (jnp.minimum(adm_end, n_keys) + kt_w - 1) // kt_w
    half_masks = [lane < HEAD_DIM, lane >= HEAD_DIM]

    def admissible(start):
        pos = start + key_row
        return (lax.shift_right_arithmetic(pos, CHUNK_SHIFT) <= q_chunk) & (pos < n_keys)

    wit = wit_ref[...]
    qi_heads = []
    for h in range(N_HEADS):
        pair = qi_ref[:, (h // 2) * LANES:(h // 2 + 1) * LANES]
        qi_heads.append(jnp.where(half_masks[h % 2], pair, 0))

    def score_block(kb, _):
        start = pl.multiple_of(kb * kt_w, kt_w)
        kt = kid_ref[pl.ds(start, kt_w), :]
        acc = jnp.zeros((kt_w, tq), F32)
        for h in range(N_HEADS):
            acc = acc + jnp.maximum(_dot_nt(kt, qi_heads[h]), 0.0) * wit[h:h + 1, :]
        acc = jnp.where(acc == 0.0, 0.0, acc)
        bits = lax.bitcast_convert_type(acc, I32)
        key = jnp.where(bits < 0, bits ^ 0x7FFFFFFF, bits)
        key_ref[kb] = jnp.where(admissible(start), key, INT_MIN)
        return 0

    lax.fori_loop(0, n_blocks, score_block, 0)

    def count(pred):
        def blk(kb, cnt):
            hit = jnp.where(pred(key_ref[kb], kb), 1.0, 0.0)
            parts = [hit[r * SUBLANES:(r + 1) * SUBLANES, :] for r in range(kt_w // SUBLANES)]
            while len(parts) > 1:
                parts = [a + b for a, b in zip(parts[::2], parts[1::2])]
            return cnt + parts[0]
        cnt = lax.fori_loop(0, n_blocks, blk, jnp.zeros((SUBLANES, tq), F32))
        return jnp.sum(cnt, axis=0, keepdims=True)

    k_f = float(top_k)
    thr = jnp.where(count(lambda key, kb: key >= 0) >= k_f, 0, INT_MIN).astype(I32)

    def bit_step(i, thr):
        cand = thr | lax.shift_left(jnp.int32(1), 30 - i)
        return jnp.where(count(lambda key, kb: key >= cand) >= k_f, cand, thr)

    thr = lax.fori_loop(0, 31, bit_step, thr)

    n_ge = count(lambda key, kb: key >= thr)
    n_cols = key_ref.shape[0] * kt_w
    cut_ref[...] = jnp.full(cut_ref.shape, n_cols, I32)

    @pl.when(jnp.max(n_ge) > k_f)
    def _():
        need = k_f - count(lambda key, kb: key > thr)

        def cut_step(i, cut):
            cand = cut | lax.shift_left(jnp.int32(1), n_cols.bit_length() - 1 - i)
            n_eq = count(lambda key, kb: (key == thr) & ((kb * kt_w + key_row) < cand))
            return jnp.where(n_eq <= need, cand, cut)

        cut = lax.fori_loop(0, n_cols.bit_length(), cut_step, jnp.zeros((1, tq), I32))
        cut_ref[...] = jnp.broadcast_to(cut, cut_ref.shape)

    cut = cut_ref[0:1, :]

    def mask_block(kb, _):
        start = kb * kt_w
        key = key_ref[kb]
        sel = (key > thr) | ((key == thr) & ((start + key_row) < cut))
        mb_ref[kb] = jnp.where(sel & admissible(start), 0.0, NEG)
        return 0

    lax.fori_loop(0, n_blocks, mask_block, 0)

    n_pairs = WIDTH // LANES
    first_half = half_masks[0]
    top_rows = lax.broadcasted_iota(I32, (LANES, 1), 0) < HEAD_DIM

    def attend(kb, carry):
        start = pl.multiple_of(kb * kt_w, kt_w)
        mb = mb_ref[kb]
        tile_d = [jnp.clip((p0 - (start + sub * LANES)) // LANES, 0, N_BIAS_TILES - 1)
                  for sub in range(kt_w // LANES)]
        out = []
        for pair in range(n_pairs):
            lanes = slice(pair * LANES, (pair + 1) * LANES)
            m, l, acc = carry[pair]
            k_bd = _head_block_diag(k_ref[pl.ds(start, kt_w), lanes], first_half)
            vt = vt_ref[lanes, pl.ds(start, kt_w)]
            vt_bd = jnp.concatenate([jnp.where(top_rows, vt, 0), jnp.where(top_rows, 0, vt)], axis=1)
            z = _dot_nt(k_bd, q_ref[:, lanes])
            m_new, alpha, p = [], [], []
            for hh in range(2):
                head = 2 * pair + hh
                bias = [tb_ref[head * N_BIAS_TILES + d][:, :tq] for d in tile_d]
                s = z[hh * kt_w:(hh + 1) * kt_w, :] + jnp.concatenate(bias, axis=0) + mb
                m_h = jnp.maximum(m[hh], jnp.max(s, axis=0, keepdims=True))
                a_h = jnp.exp(m[hh] - m_h)
                p_h = jnp.exp(s - m_h)
                l = l[:hh] + (a_h * l[hh] + jnp.sum(p_h, axis=0, keepdims=True),) + l[hh + 1:]
                m_new.append(m_h)
                alpha.append(a_h)
                p.append(p_h.astype(MXU_DTYPE))
            acc = jnp.where(top_rows, alpha[0], alpha[1]) * acc + _dot(vt_bd, jnp.concatenate(p, axis=0))
            out.append((tuple(m_new), l, acc))
        return tuple(out)

    neg_row = jnp.full((1, tq), NEG, F32)
    zero_row = jnp.zeros((1, tq), F32)
    init = tuple(((neg_row, neg_row), (zero_row, zero_row), jnp.zeros((LANES, tq), F32)) for _ in range(n_pairs))
    final = lax.fori_loop(0, n_blocks, attend, init)
    for pair in range(n_pairs):
        _, l, acc = final[pair]
        ot_ref[pair * LANES:(pair + 1) * LANES, :] = (acc / jnp.where(top_rows, l[0], l[1])).astype(ot_ref.dtype)


def _sparse_attention(q, k, v, qi, kid, wi, tiles, q_off, n_keys):
    b, t, _ = q.shape
    lp = k.shape[1]
    tq = min(Q_TILE, t)
    assert t % tq == 0 and lp % KEY_TILE == 0 and q_off % LANES == 0 and (tq == LANES or t == tq)
    top_k = min(TOPK_MAX, n_keys // 4)
    n_blocks = lp // KEY_TILE
    vt = v.transpose(0, 2, 1)
    wit = wi[:, :, :N_HEADS].transpose(0, 2, 1)
    q_spec = lambda w: pl.BlockSpec((None, tq, w), lambda bi, qi_: (bi, qi_, 0))
    k_spec = lambda w: pl.BlockSpec((None, lp, w), lambda bi, qi_: (bi, 0, 0))
    qt_spec = lambda rows: pl.BlockSpec((None, rows, tq), lambda bi, qi_: (bi, 0, qi_))
    out_t = pl.pallas_call(
        functools.partial(_sparse_kernel, q_off=q_off, tq=tq, n_keys=n_keys, top_k=top_k),
        grid=(b, t // tq),
        in_specs=[q_spec(WIDTH), k_spec(WIDTH), pl.BlockSpec((None, WIDTH, lp), lambda bi, qi_: (bi, 0, 0)),
                  q_spec(WIDTH), k_spec(LANES), qt_spec(N_HEADS),
                  pl.BlockSpec(tiles.shape, lambda bi, qi_: (0, 0, 0))],
        out_specs=qt_spec(WIDTH),
        out_shape=jax.ShapeDtypeStruct((b, WIDTH, t), MXU_DTYPE),
        scratch_shapes=[pltpu.VMEM((n_blocks, KEY_TILE, tq), I32),
                        pltpu.VMEM((n_blocks, KEY_TILE, tq), F32),
                        pltpu.VMEM((SUBLANES, tq), I32)],
        compiler_params=_params(("parallel", "arbitrary")),
        name="sparse_attention",
    )(q, k, vt, qi, kid, wit, tiles)
    return out_t.transpose(0, 2, 1)


def _ffn_kernel(x_ref, oa_ref, ob_ref, wo_ref, g2_ref, wup_ref, cw_ref, cb_ref, wdn_ref, st_ref,
                y_ref, cs_ref, x2_ref, h2_ref, acc_ref, prev_ref, *, tm, n_chunks):
    keep = CONV_WIDTH - 1

    @pl.when(pl.program_id(1) == 0)
    def _():
        prev_ref[:, :, SUBLANES - keep:, :] = st_ref[...]

    mix = jnp.concatenate([oa_ref[...], ob_ref[...]], axis=1)
    x2 = x_ref[...] + _dot(mix, wo_ref[...])
    x2_ref[...] = x2
    ms = jnp.mean(x2 * x2, axis=-1, keepdims=True)
    h2_ref[...] = (x2 * lax.rsqrt(ms + EPS) * g2_ref[...]).astype(MXU_DTYPE)
    acc_ref[...] = jnp.zeros(acc_ref.shape, F32)
    row = lax.broadcasted_iota(I32, (tm, 1), 0)

    def chunk(c, _):
        h2 = h2_ref[...]
        conv = []
        for half in range(2):
            u = _dot(h2, wup_ref[half, c])
            prev = prev_ref[half, c]
            before2, before1 = prev[SUBLANES - 2:SUBLANES - 1, :], prev[SUBLANES - 1:, :]
            u1 = jnp.where(row == 0, before1, pltpu.roll(u, 1, 0))
            u2 = jnp.where(row == 0, before2, jnp.where(row == 1, before1, pltpu.roll(u, 2, 0)))
            w = cw_ref[half, c]
            conv.append(cb_ref[half, c] + u2 * w[0:1, :] + u1 * w[1:2, :] + u * w[2:3, :])
            prev_ref[half, c] = u[tm - SUBLANES:, :]
            cs_ref[half, c] = u[tm - keep:, :]
        a, g = conv
        act = g * (1.0 / (1.0 + jnp.exp(-g))) * a
        acc_ref[...] += _dot(act.astype(MXU_DTYPE), wdn_ref[c])
        return 0

    lax.fori_loop(0, n_chunks, chunk, 0)
    y_ref[...] = x2_ref[...] + acc_ref[...]


def _ffn(x, oa, ob, w_out, g2, w_up, conv_w, conv_b, w_down, state):
    b, t, d = x.shape
    tm = min(ROW_TILE, t)
    assert t % tm == 0 and tm >= SUBLANES and CONV_WIDTH == 3
    _, n_chunks, _, fc = w_up.shape
    full = lambda a: pl.BlockSpec(a.shape, lambda bi, ti: (0,) * a.ndim)
    row_spec = lambda w: pl.BlockSpec((None, tm, w), lambda bi, ti: (bi, ti, 0))
    st_spec = pl.BlockSpec((None,) + state.shape[1:], lambda bi, ti: (bi, 0, 0, 0, 0))
    return pl.pallas_call(
        functools.partial(_ffn_kernel, tm=tm, n_chunks=n_chunks),
        grid=(b, t // tm),
        in_specs=[row_spec(d), row_spec(WIDTH), row_spec(WIDTH), full(w_out), full(g2), full(w_up),
                  full(conv_w), full(conv_b), full(w_down), st_spec],
        out_specs=[row_spec(d), st_spec],
        out_shape=[jax.ShapeDtypeStruct(x.shape, F32), jax.ShapeDtypeStruct(state.shape, F32)],
        scratch_shapes=[pltpu.VMEM((tm, d), F32), pltpu.VMEM((tm, d), MXU_DTYPE), pltpu.VMEM((tm, d), F32),
                        pltpu.VMEM((2, n_chunks, SUBLANES, fc), F32)],
        compiler_params=_params(("parallel", "arbitrary")),
        name="ffn",
    )(x, oa, ob, w_out, g2, w_up, conv_w, conv_b, w_down, state)


def _pad_keys(a, lp):
    return jnp.pad(a, ((0, 0), (0, lp - a.shape[1]), (0, 0)))


def _state_to_chunks(s, fc):
    b, keep, f2 = s.shape
    return s.reshape(b, keep, 2, f2 // 2 // fc, fc).transpose(0, 2, 3, 1, 4)


def _state_from_chunks(s):
    b, _, n_chunks, keep, fc = s.shape
    return s.transpose(0, 3, 1, 2, 4).reshape(b, keep, 2 * n_chunks * fc)


def _layer(x, caches, lw, consts):
    b, t, d = x.shape
    tri, seg, tiles = consts
    outs = _in_proj(x.reshape(b * t, d), lw["g1"], lw["w_main"], lw["w_tail"], lw["gq"], lw["gk"], seg)
    ak, av, bk, bv, ik = [o.reshape(b, t, -1) for o in outs[:5]]
    qa, ka, va, qb, kb, vb, qi, kid, wi = [o.reshape(b, t, -1) for o in outs[5:]]
    fc = lw["w_up"].shape[-1]
    if caches is None:
        past = 0
        state = jnp.zeros((b, 2, lw["w_up"].shape[1], CONV_WIDTH - 1, fc), F32)
    else:
        c_ak, c_av, c_bk, c_bv, c_ik, c_conv = caches
        past = c_ak.shape[1]
        flat = lambda c: c.reshape(b, past, -1).astype(MXU_DTYPE)
        ka = jnp.concatenate([flat(c_ak), ka], axis=1)
        va = jnp.concatenate([flat(c_av), va], axis=1)
        kb = jnp.concatenate([flat(c_bk), kb], axis=1)
        vb = jnp.concatenate([flat(c_bv), vb], axis=1)
        kid = jnp.concatenate([jnp.concatenate([flat(c_ik)] * 2, axis=-1), kid], axis=1)
        state = _state_to_chunks(c_conv, fc)
    n_keys = past + t
    lp = -(-n_keys // KEY_TILE) * KEY_TILE
    ka, va, kb, vb, kid = [_pad_keys(a, lp) for a in (ka, va, kb, vb, kid)]
    oa = _stick_breaking(qa, ka, va, tri, past)
    ob = _sparse_attention(qb, kb, vb, qi, kid, wi, tiles, past, n_keys)
    y, new_state = _ffn(x, oa, ob, lw["w_out"], lw["g2"], lw["w_up"], lw["conv_w"], lw["conv_b"],
                        lw["w_down"], state)
    heads = lambda a: a.reshape(b, t, N_HEADS, HEAD_DIM)
    return y, (heads(ak), heads(av), heads(bk), heads(bv), ik, _state_from_chunks(new_state))


def _layer_weights(l, norm1, w_in, q_norm, k_norm, w_out, norm2, w_up, conv_w, conv_b, w_down):
    d = w_in.shape[1]
    f = w_down.shape[1]
    fc = FF_TILE if f % FF_TILE == 0 else LANES
    assert f % fc == 0 and w_in.shape[2] == 7 * WIDTH + IDX_DIM + N_HEADS
    n_chunks = f // fc
    w = w_in[l]
    main = 7 * WIDTH
    w_ik, w_iw = w[:, main:main + IDX_DIM], w[:, main + IDX_DIM:]
    w_tail = jnp.concatenate([w_ik, w_ik, w_iw, jnp.zeros((d, LANES - N_HEADS), w.dtype)], axis=1)
    return {
        "g1": norm1[l][None, :],
        "w_main": w[:, :main].astype(MXU_DTYPE),
        "w_tail": w_tail.astype(MXU_DTYPE),
        "gq": jnp.tile(q_norm[l], N_HEADS)[None, :],
        "gk": jnp.tile(k_norm[l], N_HEADS)[None, :],
        "w_out": w_out[l].astype(MXU_DTYPE),
        "g2": norm2[l][None, :],
        "w_up": w_up[l].astype(MXU_DTYPE).reshape(d, 2, n_chunks, fc).transpose(1, 2, 0, 3),
        "conv_w": conv_w[l].reshape(CONV_WIDTH, 2, n_chunks, fc).transpose(1, 2, 0, 3),
        "conv_b": conv_b[l].reshape(2, n_chunks, 1, fc),
        "w_down": w_down[l].astype(MXU_DTYPE).reshape(n_chunks, fc, d),
    }


def kernel(x_prompt, x_sample, cache_a_k, cache_a_v, cache_b_k, cache_b_v, cache_idx_k, state_ffn_conv,
           rel_bias, norm1, w_in, q_norm, k_norm, w_out, norm2, w_up, conv_w, conv_b, w_down):
    depth = w_in.shape[0]
    assert cache_a_k.shape[3:] == (N_HEADS, HEAD_DIM) and cache_idx_k.shape[-1] == IDX_DIM
    r2 = jnp.arange(2 * LANES, dtype=I32)
    tri = ((r2[:, None] > r2[None, :]) & (r2[:, None] // LANES == r2[None, :] // LANES)).astype(MXU_DTYPE)
    tri = jnp.concatenate([tri, tri], axis=0)
    g = jnp.arange(WIDTH, dtype=I32) // HEAD_DIM
    seg = (g[:, None] == g[None, :]).astype(MXU_DTYPE)
    seg = jnp.concatenate([seg, seg], axis=0)
    consts = (tri, seg, _bias_tiles(rel_bias))

    yp, ys = x_prompt, x_sample
    p_states, s_states = [], []
    for l in range(depth):
        lw = _layer_weights(l, norm1, w_in, q_norm, k_norm, w_out, norm2, w_up, conv_w, conv_b, w_down)
        yp, st_p = _layer(yp, None, lw, consts)
        caches = (cache_a_k[l], cache_a_v[l], cache_b_k[l], cache_b_v[l], cache_idx_k[l], state_ffn_conv[l])
        ys, st_s = _layer(ys, caches, lw, consts)
        p_states.append(st_p)
        s_states.append(st_s)
    stack = lambda sts, i: jnp.stack([s[i] for s in sts], axis=0)
    return (yp, ys) + tuple(stack(p_states, i) for i in range(6)) + tuple(stack(s_states, i) for i in range(6))
```

```python
import functools
import math

import jax
import jax.numpy as jnp
from jax import lax
from jax.experimental import pallas as pl
from jax.experimental.pallas import tpu as pltpu

HEAD_DIM = 64
N_HEADS = 8
IDX_DIM = 64
WIDTH = N_HEADS * HEAD_DIM
CHUNK = 64
CHUNK_SHIFT = 6
TOPK_MAX = 256
N_BUCKETS = 32
T5_MAX_DISTANCE = 128
CONV_WIDTH = 3
EPS = 1e-6

LANES = 128
SUBLANES = 8
Q_TILE = 128
SB_Q_TILE = 256
SB_KEY_TILE = 256
SB_ROW_CHUNK = 32
SB_VANISH_LOG = -105.0
KEY_TILE = 512
ROW_TILE = 256
FF_TILE = 256
VMEM_LIMIT_BYTES = 56 * 1024 * 1024

MXU_DTYPE = jnp.bfloat16
F32 = jnp.float32
I32 = jnp.int32
I16 = jnp.int16
INT_MIN = -(2 ** 31)
HALF_BIAS = 2 ** 15
NEG = -1e30
LOG2E = 1.0 / math.log(2.0)

_NT = (((1,), (1,)), ((), ()))


def _dot(a, b):
    return jnp.dot(a, b, preferred_element_type=F32)


def _dot_nt(a, b):
    return lax.dot_general(a, b, _NT, preferred_element_type=F32)


def _split_hi_lo(x):
    hi = x.astype(MXU_DTYPE)
    lo = (x - hi.astype(F32)).astype(MXU_DTYPE)
    return jnp.concatenate([hi, lo], axis=1)


def _params(semantics):
    return pltpu.CompilerParams(dimension_semantics=semantics, vmem_limit_bytes=VMEM_LIMIT_BYTES)


def _in_proj_kernel(x_ref, g1_ref, w_ref, wt_ref, gq_ref, gk_ref, seg_ref,
                    ak_ref, av_ref, bk_ref, bv_ref, ik_ref,
                    qa_ref, ka_ref, va_ref, qb_ref, kb_ref, vb_ref, qi_ref, kid_ref, wi_ref):
    x = x_ref[...]
    ms = jnp.mean(x * x, axis=-1, keepdims=True)
    h = (x * lax.rsqrt(ms + EPS) * g1_ref[...]).astype(MXU_DTYPE)

    def proj(j):
        return _dot(h, w_ref[:, j * WIDTH:(j + 1) * WIDTH])

    def head_norm(p, g_ref):
        ss = _dot(_split_hi_lo(p * p), seg_ref[...])
        return p * lax.rsqrt(ss * (1.0 / HEAD_DIM) + EPS) * g_ref[...]

    scale = HEAD_DIM ** -0.5
    qa_ref[...] = (proj(0) * scale).astype(MXU_DTYPE)
    p = proj(1)
    ak_ref[...] = p
    ka_ref[...] = p.astype(MXU_DTYPE)
    p = proj(2)
    av_ref[...] = p
    va_ref[...] = p.astype(MXU_DTYPE)
    qb_ref[...] = (head_norm(proj(3), gq_ref) * (scale * LOG2E)).astype(MXU_DTYPE)
    p = head_norm(proj(4), gk_ref)
    bk_ref[...] = p
    kb_ref[...] = p.astype(MXU_DTYPE)
    p = proj(5)
    bv_ref[...] = p
    vb_ref[...] = p.astype(MXU_DTYPE)
    qi_ref[...] = proj(6).astype(MXU_DTYPE)
    t = _dot(h, wt_ref[...])
    ik_ref[...] = t[:, :IDX_DIM]
    kid_ref[...] = t[:, :LANES].astype(MXU_DTYPE)
    wi_ref[...] = t[:, LANES:] * (IDX_DIM ** -0.5 * N_HEADS ** -0.5)


def _in_proj(x2d, g1, w_main, w_tail, gq, gk, seg):
    rows, d = x2d.shape
    tm = min(ROW_TILE, rows)
    assert rows % tm == 0
    row_spec = lambda w: pl.BlockSpec((tm, w), lambda i: (i, 0))
    full = lambda a: pl.BlockSpec(a.shape, lambda i: (0,) * a.ndim)
    f32_out = lambda w: jax.ShapeDtypeStruct((rows, w), F32)
    mx_out = lambda w: jax.ShapeDtypeStruct((rows, w), MXU_DTYPE)
    return pl.pallas_call(
        _in_proj_kernel,
        grid=(rows // tm,),
        in_specs=[row_spec(d), full(g1), full(w_main), full(w_tail), full(gq), full(gk), full(seg)],
        out_specs=[row_spec(WIDTH)] * 4 + [row_spec(IDX_DIM)] + [row_spec(WIDTH)] * 7
                  + [row_spec(LANES), row_spec(LANES)],
        out_shape=[f32_out(WIDTH)] * 4 + [f32_out(IDX_DIM)] + [mx_out(WIDTH)] * 7
                  + [mx_out(LANES), f32_out(LANES)],
        compiler_params=_params(("parallel",)),
        name="in_proj",
    )(x2d, g1, w_main, w_tail, gq, gk, seg)


def _head_block_diag(x, first_half):
    return jnp.concatenate([jnp.where(first_half, x, 0), jnp.where(first_half, 0, x)], axis=0)


def _stick_breaking_kernel(q_ref, k_ref, v_ref, tri_ref, o_ref, lb_ref, hl_ref, w_ref, *, q_off, tq):
    p0 = q_off + pl.program_id(2) * tq
    q = q_ref[...]
    lane = lax.broadcasted_iota(I32, (1, LANES), 1)
    lane2 = lax.broadcasted_iota(I32, (1, 2 * LANES), 1)
    first_half = lane < HEAD_DIM
    key_off = lane2 & (LANES - 1)
    row_pos = p0 + lax.broadcasted_iota(I32, (tq, 1), 0)
    n_blocks = (p0 + tq - 1 + LANES - 1) // LANES
    n_diag = n_blocks - p0 // LANES
    tri = tri_ref[...]

    def block_start(j):
        return pl.multiple_of(jnp.clip(n_blocks - 1 - j, 0, n_blocks - 1) * LANES, LANES)

    row_chunks = [slice(r, r + min(tq, SB_ROW_CHUNK)) for r in range(0, tq, min(tq, SB_ROW_CHUNK))]
    one = jnp.minimum(n_blocks, 1).astype(F32)

    def half_step(masked, j, slot, carry):
        acc, later0, later1, later0_p, later1_p = carry
        other = 1 - slot
        start = block_start(j)
        k_bd = _head_block_diag(k_ref[pl.ds(start, LANES), :], first_half)
        z = _dot_nt(q, k_bd)
        if masked:
            key_pos = jnp.where(j < n_blocks, start, q_off + tq * pl.num_programs(2)) + key_off
        sums = ([], [])
        for rows in row_chunks:
            zc = z[rows, :]
            log_beta = jnp.minimum(zc, 0.0) - jnp.log(one + jnp.exp(-jnp.abs(zc)))
            log_1mb = log_beta - zc
            if masked:
                causal = key_pos < row_pos[rows, :]
                log_1mb = jnp.where(causal, log_1mb, 0.0)
                log_beta = jnp.where(causal, log_beta, NEG)
            hi = log_1mb.astype(MXU_DTYPE)
            lb_ref[slot, rows, :] = log_beta
            hl_ref[slot, rows, :2 * LANES] = hi
            hl_ref[slot, rows, 2 * LANES:] = (log_1mb - hi.astype(F32)).astype(MXU_DTYPE)
            sums[0].append(jnp.sum(log_1mb[:, :LANES], axis=1, keepdims=True))
            sums[1].append(jnp.sum(log_1mb[:, LANES:], axis=1, keepdims=True))
        new_later0 = later0 + jnp.concatenate(sums[0], axis=0)
        new_later1 = later1 + jnp.concatenate(sums[1], axis=0)
        w_pp = w_ref[slot]
        within = _dot(hl_ref[other], tri)
        for rows in row_chunks:
            n = rows.stop - rows.start
            later = jnp.concatenate([jnp.broadcast_to(later0_p[rows, :], (n, LANES)),
                                     jnp.broadcast_to(later1_p[rows, :], (n, LANES))], axis=1)
            w_ref[other, rows, :] = jnp.exp(lb_ref[other, rows, :] + (within[rows, :] + later)).astype(MXU_DTYPE)
        v_bd = _head_block_diag(v_ref[pl.ds(block_start(j - 2), LANES), :], first_half)
        acc = acc + _dot(w_pp, v_bd)
        return acc, new_later0, new_later1, later0, later1

    def pair_step(masked, p, carry):
        carry = half_step(masked, 2 * p, 0, carry)
        return half_step(masked, 2 * p + 1, 1, carry)

    lb_ref[...] = jnp.full(lb_ref.shape, NEG, F32)
    hl_ref[...] = jnp.zeros(hl_ref.shape, MXU_DTYPE)
    w_ref[...] = jnp.zeros(w_ref.shape, MXU_DTYPE)
    zero_col = row_pos.astype(F32) * 0.0
    carry = (jnp.zeros((tq, LANES), F32), zero_col, zero_col, zero_col, zero_col)
    pairs_diag = (n_diag + 1) // 2
    pairs_full = jnp.maximum(n_blocks // 2, pairs_diag)
    pairs_all = (n_blocks + 2 + 1) // 2
    def weights_vanished(carry):
        return (jnp.maximum(jnp.max(carry[1]), jnp.max(carry[2])) < SB_VANISH_LOG).astype(I32)

    def unmasked_pair(state):
        p, _, carry = state
        carry = pair_step(False, p, carry)
        return p + 1, weights_vanished(carry), carry

    carry = lax.fori_loop(0, pairs_diag, functools.partial(pair_step, True), carry)
    p, vanished, carry = lax.while_loop(lambda s: (s[0] < pairs_full) & (s[1] == 0), unmasked_pair,
                                        (pairs_diag, weights_vanished(carry), carry))
    carry = lax.fori_loop(p, jnp.where(vanished == 1, p + 1, pairs_all), functools.partial(pair_step, True), carry)
    o_ref[...] = carry[0].astype(o_ref.dtype)


def _stick_breaking(q, k, v, tri, q_off):
    b, t, _ = q.shape
    lp = k.shape[1]
    tq = min(SB_Q_TILE, t)
    assert t % tq == 0 and lp % SB_KEY_TILE == 0
    pairs = WIDTH // LANES
    kv_spec = pl.BlockSpec((None, lp, LANES), lambda bi, pi, qi: (bi, 0, pi))
    q_spec = pl.BlockSpec((None, tq, LANES), lambda bi, pi, qi: (bi, qi, pi))
    return pl.pallas_call(
        functools.partial(_stick_breaking_kernel, q_off=q_off, tq=tq),
        grid=(b, pairs, t // tq),
        in_specs=[q_spec, kv_spec, kv_spec, pl.BlockSpec(tri.shape, lambda bi, pi, qi: (0, 0))],
        out_specs=q_spec,
        out_shape=jax.ShapeDtypeStruct(q.shape, MXU_DTYPE),
        scratch_shapes=[pltpu.VMEM((2, tq, 2 * LANES), F32),
                        pltpu.VMEM((2, tq, 4 * LANES), MXU_DTYPE),
                        pltpu.VMEM((2, tq, 2 * LANES), MXU_DTYPE)],
        compiler_params=_params(("parallel", "parallel", "arbitrary")),
        name="stick_breaking",
    )(q, k, v, tri)


def _bias_tiles_kernel(rb_ref, bucket_ref, tb_ref):
    h = pl.program_id(0)
    tiles = []
    for d in range(bucket_ref.shape[0]):
        bucket = bucket_ref[d]
        acc = jnp.zeros(bucket.shape, F32)
        for j in range(N_BUCKETS):
            acc = jnp.where(bucket == j, rb_ref[j, h], acc)
        tiles.append(acc)
    for d, tile in enumerate(tiles):
        tb_ref[d] = (tile - tiles[-1]) * LOG2E


def _t5_bucket(rel):
    half = N_BUCKETS // 2
    max_exact = half // 2
    base = jnp.where(rel > 0, half, 0)
    n = jnp.abs(rel)
    large = max_exact + (jnp.log(jnp.maximum(n, 1).astype(jnp.float32) / max_exact)
                         / math.log(T5_MAX_DISTANCE / max_exact) * (half - max_exact)).astype(jnp.int32)
    large = jnp.minimum(large, half - 1)
    return base + jnp.where(n < max_exact, n, large)


N_BIAS_TILES = 3
N_NEAR_TILES = 2


def _bias_tiles(rel_bias):
    assert T5_MAX_DISTANCE <= LANES
    j = jnp.arange(LANES, dtype=I32)[None, :, None]
    i = jnp.arange(LANES, dtype=I32)[None, None, :]
    d = jnp.arange(N_BIAS_TILES, dtype=I32)[:, None, None]
    bucket = _t5_bucket(j - i - LANES * d).astype(I32)
    return pl.pallas_call(
        _bias_tiles_kernel,
        grid=(N_HEADS,),
        in_specs=[pl.BlockSpec(memory_space=pltpu.SMEM),
                  pl.BlockSpec(bucket.shape, lambda h: (0, 0, 0))],
        out_specs=pl.BlockSpec((N_BIAS_TILES, LANES, LANES), lambda h: (h, 0, 0)),
        out_shape=jax.ShapeDtypeStruct((N_HEADS * N_BIAS_TILES, LANES, LANES), F32),
        compiler_params=_params(("arbitrary",)),
        name="bias_tiles",
    )(rel_bias, bucket)


def _sparse_kernel(q_ref, k_ref, vt_ref, qi_ref, kid_ref, wit_ref, tb_ref, ot_ref,
                   key_ref, mb_ref, cut_ref, hi16_ref, lo16_ref, *, q_off, tq, n_keys, top_k):
    kt_w = KEY_TILE
    p0 = q_off + pl.program_id(1) * tq
    lane = lax.broadcasted_iota(I32, (1, LANES), 1)
    key_row = lax.broadcasted_iota(I32, (kt_w, tq), 0)
    q_chunk = lax.shift_right_arithmetic(p0 + lax.broadcasted_iota(I32, (1, tq), 1), CHUNK_SHIFT)
    adm_end = (lax.shift_right_arithmetic(p0 + tq - 1, CHUNK_SHIFT) + 1) * CHUNK
    n_blocks = (jnp.minimum(adm_end, n_keys) + kt_w - 1) // kt_w
    half_masks = [lane < HEAD_DIM, lane >= HEAD_DIM]

    def admissible(start):
        pos = start + key_row
        return (lax.shift_right_arithmetic(pos, CHUNK_SHIFT) <= q_chunk) & (pos < n_keys)

    wit = wit_ref[...]
    qi_heads = []
    for h in range(N_HEADS):
        pair = qi_ref[:, (h // 2) * LANES:(h // 2 + 1) * LANES]
        qi_heads.append(jnp.where(half_masks[h % 2], pair, 0))

    def score_block(kb, _):
        start = pl.multiple_of(kb * kt_w, kt_w)
        kt = kid_ref[pl.ds(start, kt_w), :]
        acc = jnp.zeros((kt_w, tq), F32)
        for h in range(N_HEADS):
            acc = acc + jnp.maximum(_dot_nt(kt, qi_heads[h]), 0.0) * wit[h:h + 1, :]
        acc = jnp.where(acc == 0.0, 0.0, acc)
        bits = lax.bitcast_convert_type(acc, I32)
        key = jnp.where(bits < 0, bits ^ 0x7FFFFFFF, bits)
        key = jnp.where(admissible(start), key, INT_MIN)
        key_ref[kb] = key
        hi16_ref[kb] = lax.shift_right_arithmetic(key, 16).astype(I16)
        lo16_ref[kb] = ((key & 0xFFFF) - HALF_BIAS).astype(I16)
        return 0

    lax.fori_loop(0, n_blocks, score_block, 0)

    def count16(ref, pred):
        rows = 2 * SUBLANES
        def blk(kb, cnt):
            hit = jnp.where(pred(ref[kb]), jnp.ones((), I16), jnp.zeros((), I16))
            parts = [hit[r * rows:(r + 1) * rows, :] for r in range(kt_w // rows)]
            while len(parts) > 1:
                parts = [a + b for a, b in zip(parts[::2], parts[1::2])]
            return cnt + parts[0]
        cnt = lax.fori_loop(0, n_blocks, blk, jnp.zeros((rows, tq), I16))
        return jnp.sum(cnt.astype(I32), axis=0, keepdims=True)

    def count(pred):
        def blk(kb, cnt):
            hit = jnp.where(pred(key_ref[kb], kb), 1.0, 0.0)
            parts = [hit[r * SUBLANES:(r + 1) * SUBLANES, :] for r in range(kt_w // SUBLANES)]
            while len(parts) > 1:
                parts = [a + b for a, b in zip(parts[::2], parts[1::2])]
            return cnt + parts[0]
        cnt = lax.fori_loop(0, n_blocks, blk, jnp.zeros((SUBLANES, tq), F32))
        return jnp.sum(cnt, axis=0, keepdims=True)

    k_f = float(top_k)
    thr_hi = jnp.where(count16(hi16_ref, lambda h: h >= 0) >= top_k, 0, -HALF_BIAS).astype(I32)

    def hi_step(i, thr_hi):
        cand = thr_hi | lax.shift_left(jnp.int32(1), 14 - i)
        cand16 = cand.astype(I16)
        return jnp.where(count16(hi16_ref, lambda h: h >= cand16) >= top_k, cand, thr_hi)

    thr_hi = lax.fori_loop(0, 15, hi_step, thr_hi)
    thr_hi16 = thr_hi.astype(I16)
    need_lo = top_k - count16(hi16_ref, lambda h: h > thr_hi16)

    def keep_tied(kb, _):
        lo16_ref[kb] = jnp.where(hi16_ref[kb] == thr_hi16, lo16_ref[kb], jnp.full((), -HALF_BIAS, I16))
        return 0

    lax.fori_loop(0, n_blocks, keep_tied, 0)

    def lo_step(i, thr_lo):
        cand = thr_lo | lax.shift_left(jnp.int32(1), 15 - i)
        cand16 = (cand - HALF_BIAS).astype(I16)
        return jnp.where(count16(lo16_ref, lambda l: l >= cand16) >= need_lo, cand, thr_lo)

    thr_lo = lax.fori_loop(0, 16, lo_step, jnp.zeros((1, tq), I32))
    thr = lax.shift_left(thr_hi, 16) | thr_lo

    n_ge = count(lambda key, kb: key >= thr)
    n_cols = key_ref.shape[0] * kt_w
    cut_ref[...] = jnp.full(cut_ref.shape, n_cols, I32)

    @pl.when(jnp.max(n_ge) > k_f)
    def _():
        need = k_f - count(lambda key, kb: key > thr)

        def cut_step(i, cut):
            cand = cut | lax.shift_left(jnp.int32(1), n_cols.bit_length() - 1 - i)
            n_eq = count(lambda key, kb: (key == thr) & ((kb * kt_w + key_row) < cand))
            return jnp.where(n_eq <= need, cand, cut)

        cut = lax.fori_loop(0, n_cols.bit_length(), cut_step, jnp.zeros((1, tq), I32))
        cut_ref[...] = jnp.broadcast_to(cut, cut_ref.shape)

    cut = cut_ref[0:1, :]

    def mask_block(kb, _):
        start = kb * kt_w
        key = key_ref[kb]
        sel = (key > thr) | ((key == thr) & ((start + key_row) < cut))
        mb_ref[kb] = jnp.where(sel & admissible(start), 0.0, NEG)
        return 0

    lax.fori_loop(0, n_blocks, mask_block, 0)

    n_pairs = WIDTH // LANES
    first_half = half_masks[0]
    top_rows = lax.broadcasted_iota(I32, (LANES, 1), 0) < HEAD_DIM

    def attend(near, kb, carry):
        start = pl.multiple_of(kb * kt_w, kt_w)
        mb = mb_ref[kb]
        tile_d = [jnp.clip((p0 - (start + sub * LANES)) // LANES, 0, N_BIAS_TILES - 1)
                  for sub in range(kt_w // LANES)]
        out = []
        for pair in range(n_pairs):
            lanes = slice(pair * LANES, (pair + 1) * LANES)
            m, l, acc = carry[pair]
            k_bd = _head_block_diag(k_ref[pl.ds(start, kt_w), lanes], first_half)
            vt = vt_ref[lanes, pl.ds(start, kt_w)]
            vt_bd = jnp.concatenate([jnp.where(top_rows, vt, 0), jnp.where(top_rows, 0, vt)], axis=1)
            z = _dot_nt(k_bd, q_ref[:, lanes])
            m_new, alpha, p = [], [], []
            for hh in range(2):
                head = 2 * pair + hh
                s = z[hh * kt_w:(hh + 1) * kt_w, :] + mb
                if near:
                    bias = [tb_ref[head * N_BIAS_TILES + d][:, :tq] for d in tile_d]
                    s = s + jnp.concatenate(bias, axis=0)
                m_h = jnp.maximum(m[hh], jnp.max(s, axis=0, keepdims=True))
                a_h = jnp.exp2(m[hh] - m_h)
                p_h = jnp.exp2(s - m_h)
                l = l[:hh] + (a_h * l[hh] + jnp.sum(p_h, axis=0, keepdims=True),) + l[hh + 1:]
                m_new.append(m_h)
                alpha.append(a_h)
                p.append(p_h.astype(MXU_DTYPE))
            acc = jnp.where(top_rows, alpha[0], alpha[1]) * acc + _dot(vt_bd, jnp.concatenate(p, axis=0))
            out.append((tuple(m_new), l, acc))
        return tuple(out)

    neg_row = jnp.full((1, tq), NEG, F32)
    zero_row = jnp.zeros((1, tq), F32)
    init = tuple(((neg_row, neg_row), (zero_row, zero_row), jnp.zeros((LANES, tq), F32)) for _ in range(n_pairs))
    n_far = jnp.maximum(n_blocks - N_NEAR_TILES, 0)
    carry = lax.fori_loop(0, n_far, functools.partial(attend, False), init)
    final = lax.fori_loop(n_far, n_blocks, functools.partial(attend, True), carry)
    for pair in range(n_pairs):
        _, l, acc = final[pair]
        ot_ref[pair * LANES:(pair + 1) * LANES, :] = (acc / jnp.where(top_rows, l[0], l[1])).astype(ot_ref.dtype)


def _sparse_attention(q, k, v, qi, kid, wi, tiles, q_off, n_keys):
    b, t, _ = q.shape
    lp = k.shape[1]
    tq = min(Q_TILE, t)
    assert t % tq == 0 and lp % KEY_TILE == 0 and q_off % LANES == 0 and (tq == LANES or t == tq)
    top_k = min(TOPK_MAX, n_keys // 4)
    n_blocks = lp // KEY_TILE
    assert lp // (2 * SUBLANES) < HALF_BIAS
    vt = v.transpose(0, 2, 1)
    wit = wi[:, :, :N_HEADS].transpose(0, 2, 1)
    q_spec = lambda w: pl.BlockSpec((None, tq, w), lambda bi, qi_: (bi, qi_, 0))
    k_spec = lambda w: pl.BlockSpec((None, lp, w), lambda bi, qi_: (bi, 0, 0))
    qt_spec = lambda rows: pl.BlockSpec((None, rows, tq), lambda bi, qi_: (bi, 0, qi_))
    out_t = pl.pallas_call(
        functools.partial(_sparse_kernel, q_off=q_off, tq=tq, n_keys=n_keys, top_k=top_k),
        grid=(b, t // tq),
        in_specs=[q_spec(WIDTH), k_spec(WIDTH), pl.BlockSpec((None, WIDTH, lp), lambda bi, qi_: (bi, 0, 0)),
                  q_spec(WIDTH), k_spec(LANES), qt_spec(N_HEADS),
                  pl.BlockSpec(tiles.shape, lambda bi, qi_: (0, 0, 0))],
        out_specs=qt_spec(WIDTH),
        out_shape=jax.ShapeDtypeStruct((b, WIDTH, t), MXU_DTYPE),
        scratch_shapes=[pltpu.VMEM((n_blocks, KEY_TILE, tq), I32),
                        pltpu.VMEM((n_blocks, KEY_TILE, tq), F32),
                        pltpu.VMEM((SUBLANES, tq), I32),
                        pltpu.VMEM((n_blocks, KEY_TILE, tq), I16),
                        pltpu.VMEM((n_blocks, KEY_TILE, tq), I16)],
        compiler_params=_params(("parallel", "arbitrary")),
        name="sparse_attention",
    )(q, k, vt, qi, kid, wit, tiles)
    return out_t.transpose(0, 2, 1)


def _ffn_kernel(x_ref, oa_ref, ob_ref, wo_ref, g2_ref, wup_ref, cw_ref, cb_ref, wdn_ref, st_ref,
                y_ref, cs_ref, x2_ref, h2_ref, acc_ref, prev_ref, u_ref, *, tm, n_chunks):
    keep = CONV_WIDTH - 1

    @pl.when(pl.program_id(1) == 0)
    def _():
        prev_ref[:, :, SUBLANES - keep:, :] = st_ref[...]

    mix = jnp.concatenate([oa_ref[...], ob_ref[...]], axis=1)
    x2 = x_ref[...] + _dot(mix, wo_ref[...])
    x2_ref[...] = x2
    ms = jnp.mean(x2 * x2, axis=-1, keepdims=True)
    h2_ref[...] = (x2 * lax.rsqrt(ms + EPS) * g2_ref[...]).astype(MXU_DTYPE)
    acc_ref[...] = jnp.zeros(acc_ref.shape, F32)
    row = lax.broadcasted_iota(I32, (tm, 1), 0)

    def up_proj(c, slot):
        h2 = h2_ref[...]
        for half in range(2):
            u_ref[slot, half] = _dot(h2, wup_ref[half, c])

    def finish_chunk(c, slot):
        conv = []
        for half in range(2):
            u = u_ref[slot, half]
            prev = prev_ref[half, c]
            before2, before1 = prev[SUBLANES - 2:SUBLANES - 1, :], prev[SUBLANES - 1:, :]
            u1 = jnp.where(row == 0, before1, pltpu.roll(u, 1, 0))
            u2 = jnp.where(row == 0, before2, jnp.where(row == 1, before1, pltpu.roll(u, 2, 0)))
            w = cw_ref[half, c]
            conv.append(cb_ref[half, c] + u2 * w[0:1, :] + u1 * w[1:2, :] + u * w[2:3, :])
            prev_ref[half, c] = u[tm - SUBLANES:, :]
            cs_ref[half, c] = u[tm - keep:, :]
        a, g = conv
        act = g * (1.0 / (1.0 + jnp.exp(-g))) * a
        acc_ref[...] += _dot(act.astype(MXU_DTYPE), wdn_ref[c])

    def chunk_pair(p, _):
        c = 2 * p
        up_proj(c + 1, 1)
        finish_chunk(c, 0)
        up_proj(c + 2, 0)
        finish_chunk(c + 1, 1)
        return 0

    up_proj(0, 0)
    n_pairs = (n_chunks - 1) // 2
    lax.fori_loop(0, n_pairs, chunk_pair, 0)
    last = 2 * n_pairs
    if last + 1 < n_chunks:
        up_proj(last + 1, 1)
        finish_chunk(last, 0)
        finish_chunk(last + 1, 1)
    else:
        finish_chunk(last, 0)
    y_ref[...] = x2_ref[...] + acc_ref[...]


def _ffn(x, oa, ob, w_out, g2, w_up, conv_w, conv_b, w_down, state):
    b, t, d = x.shape
    tm = min(ROW_TILE, t)
    assert t % tm == 0 and tm >= SUBLANES and CONV_WIDTH == 3
    _, n_chunks, _, fc = w_up.shape
    full = lambda a: pl.BlockSpec(a.shape, lambda bi, ti: (0,) * a.ndim)
    row_spec = lambda w: pl.BlockSpec((None, tm, w), lambda bi, ti: (bi, ti, 0))
    st_spec = pl.BlockSpec((None,) + state.shape[1:], lambda bi, ti: (bi, 0, 0, 0, 0))
    return pl.pallas_call(
        functools.partial(_ffn_kernel, tm=tm, n_chunks=n_chunks),
        grid=(b, t // tm),
        in_specs=[row_spec(d), row_spec(WIDTH), row_spec(WIDTH), full(w_out), full(g2), full(w_up),
                  full(conv_w), full(conv_b), full(w_down), st_spec],
        out_specs=[row_spec(d), st_spec],
        out_shape=[jax.ShapeDtypeStruct(x.shape, F32), jax.ShapeDtypeStruct(state.shape, F32)],
        scratch_shapes=[pltpu.VMEM((tm, d), F32), pltpu.VMEM((tm, d), MXU_DTYPE), pltpu.VMEM((tm, d), F32),
                        pltpu.VMEM((2, n_chunks, SUBLANES, fc), F32),
                        pltpu.VMEM((2, 2, tm, fc), F32)],
        compiler_params=_params(("parallel", "arbitrary")),
        name="ffn",
    )(x, oa, ob, w_out, g2, w_up, conv_w, conv_b, w_down, state)


def _pad_keys(a, lp):
    return jnp.pad(a, ((0, 0), (0, lp - a.shape[1]), (0, 0)))


def _state_to_chunks(s, fc):
    b, keep, f2 = s.shape
    return s.reshape(b, keep, 2, f2 // 2 // fc, fc).transpose(0, 2, 3, 1, 4)


def _state_from_chunks(s):
    b, _, n_chunks, keep, fc = s.shape
    return s.transpose(0, 3, 1, 2, 4).reshape(b, keep, 2 * n_chunks * fc)


def _layer(x, caches, lw, consts):
    b, t, d = x.shape
    tri, seg, tiles = consts
    outs = _in_proj(x.reshape(b * t, d), lw["g1"], lw["w_main"], lw["w_tail"], lw["gq"], lw["gk"], seg)
    ak, av, bk, bv, ik = [o.reshape(b, t, -1) for o in outs[:5]]
    qa, ka, va, qb, kb, vb, qi, kid, wi = [o.reshape(b, t, -1) for o in outs[5:]]
    fc = lw["w_up"].shape[-1]
    if caches is None:
        past = 0
        state = jnp.zeros((b, 2, lw["w_up"].shape[1], CONV_WIDTH - 1, fc), F32)
    else:
        c_ak, c_av, c_bk, c_bv, c_ik, c_conv = caches
        past = c_ak.shape[1]
        flat = lambda c: c.reshape(b, past, -1).astype(MXU_DTYPE)
        ka = jnp.concatenate([flat(c_ak), ka], axis=1)
        va = jnp.concatenate([flat(c_av), va], axis=1)
        kb = jnp.concatenate([flat(c_bk), kb], axis=1)
        vb = jnp.concatenate([flat(c_bv), vb], axis=1)
        kid = jnp.concatenate([jnp.concatenate([flat(c_ik)] * 2, axis=-1), kid], axis=1)
        state = _state_to_chunks(c_conv, fc)
    n_keys = past + t
    lp = -(-n_keys // KEY_TILE) * KEY_TILE
    ka, va, kb, vb, kid = [_pad_keys(a, lp) for a in (ka, va, kb, vb, kid)]
    oa = _stick_breaking(qa, ka, va, tri, past)
    ob = _sparse_attention(qb, kb, vb, qi, kid, wi, tiles, past, n_keys)
    y, new_state = _ffn(x, oa, ob, lw["w_out"], lw["g2"], lw["w_up"], lw["conv_w"], lw["conv_b"],
                        lw["w_down"], state)
    heads = lambda a: a.reshape(b, t, N_HEADS, HEAD_DIM)
    return y, (heads(ak), heads(av), heads(bk), heads(bv), ik, _state_from_chunks(new_state))


def _layer_weights(l, norm1, w_in, q_norm, k_norm, w_out, norm2, w_up, conv_w, conv_b, w_down):
    d = w_in.shape[1]
    f = w_down.shape[1]
    fc = FF_TILE if f % FF_TILE == 0 else LANES
    assert f % fc == 0 and w_in.shape[2] == 7 * WIDTH + IDX_DIM + N_HEADS
    n_chunks = f // fc
    w = w_in[l]
    main = 7 * WIDTH
    w_ik, w_iw = w[:, main:main + IDX_DIM], w[:, main + IDX_DIM:]
    w_tail = jnp.concatenate([w_ik, w_ik, w_iw, jnp.zeros((d, LANES - N_HEADS), w.dtype)], axis=1)
    return {
        "g1": norm1[l][None, :],
        "w_main": w[:, :main].astype(MXU_DTYPE),
        "w_tail": w_tail.astype(MXU_DTYPE),
        "gq": jnp.tile(q_norm[l], N_HEADS)[None, :],
        "gk": jnp.tile(k_norm[l], N_HEADS)[None, :],
        "w_out": w_out[l].astype(MXU_DTYPE),
        "g2": norm2[l][None, :],
        "w_up": w_up[l].astype(MXU_DTYPE).reshape(d, 2, n_chunks, fc).transpose(1, 2, 0, 3),
        "conv_w": conv_w[l].reshape(CONV_WIDTH, 2, n_chunks, fc).transpose(1, 2, 0, 3),
        "conv_b": conv_b[l].reshape(2, n_chunks, 1, fc),
        "w_down": w_down[l].astype(MXU_DTYPE).reshape(n_chunks, fc, d),
    }


def kernel(x_prompt, x_sample, cache_a_k, cache_a_v, cache_b_k, cache_b_v, cache_idx_k, state_ffn_conv,
           rel_bias, norm1, w_in, q_norm, k_norm, w_out, norm2, w_up, conv_w, conv_b, w_down):
    depth = w_in.shape[0]
    assert cache_a_k.shape[3:] == (N_HEADS, HEAD_DIM) and cache_idx_k.shape[-1] == IDX_DIM
    r2 = jnp.arange(2 * LANES, dtype=I32)
    tri = ((r2[:, None] > r2[None, :]) & (r2[:, None] // LANES == r2[None, :] // LANES)).astype(MXU_DTYPE)
    tri = jnp.concatenate([tri, tri], axis=0)
    g = jnp.arange(WIDTH, dtype=I32) // HEAD_DIM
    seg = (g[:, None] == g[None, :]).astype(MXU_DTYPE)
    seg = jnp.concatenate([seg, seg], axis=0)
    consts = (tri, seg, _bias_tiles(rel_bias))

    yp, ys = x_prompt, x_sample
    p_states, s_states = [], []
    for l in range(depth):
        lw = _layer_weights(l, norm1, w_in, q_norm, k_norm, w_out, norm2, w_up, conv_w, conv_b, w_down)
        yp, st_p = _layer(yp, None, lw, consts)
        caches = (cache_a_k[l], cache_a_v[l], cache_b_k[l], cache_b_v[l], cache_idx_k[l], state_ffn_conv[l])
        ys, st_s = _layer(ys, caches, lw, consts)
        p_states.append(st_p)
        s_states.append(st_s)
    stack = lambda sts, i: jnp.stack([s[i] for s in sts], axis=0)
    return (yp, ys) + tuple(stack(p_states, i) for i in range(6)) + tuple(stack(s_states, i) for i in range(6))
```

```python
import functools
import math

import jax
import jax.numpy as jnp
from jax import lax
from jax.experimental import pallas as pl
from jax.experimental.pallas import tpu as pltpu

HEAD_DIM = 64
N_HEADS = 8
IDX_DIM = 64
WIDTH = N_HEADS * HEAD_DIM
CHUNK = 64
CHUNK_SHIFT = 6
TOPK_MAX = 256
N_BUCKETS = 32
T5_MAX_DISTANCE = 128
CONV_WIDTH = 3
EPS = 1e-6

LANES = 128
SUBLANES = 8
Q_TILE = 256
COUNT_ROWS = 64
SB_Q_TILE = 256
SB_KEY_TILE = 256
SB_ROW_CHUNK = 32
SB_VANISH_LOG = -105.0
KEY_TILE = 512
ROW_TILE = 256
FF_TILE = 256
VMEM_LIMIT_BYTES = 56 * 1024 * 1024

MXU_DTYPE = jnp.bfloat16
F32 = jnp.float32
I32 = jnp.int32
INT_MIN = -(2 ** 31)
NEG = -1e30
LOG2E = 1.0 / math.log(2.0)

_NT = (((1,), (1,)), ((), ()))


def _dot(a, b):
    return jnp.dot(a, b, preferred_element_type=F32)


def _dot_nt(a, b):
    return lax.dot_general(a, b, _NT, preferred_element_type=F32)


def _split_hi_lo(x):
    hi = x.astype(MXU_DTYPE)
    lo = (x - hi.astype(F32)).astype(MXU_DTYPE)
    return jnp.concatenate([hi, lo], axis=1)


def _params(semantics):
    return pltpu.CompilerParams(dimension_semantics=semantics, vmem_limit_bytes=VMEM_LIMIT_BYTES)


def _in_proj_kernel(x_ref, g1_ref, w_ref, wt_ref, gq_ref, gk_ref, seg_ref,
                    ak_ref, av_ref, bk_ref, bv_ref, ik_ref,
                    qa_ref, ka_ref, va_ref, qb_ref, kb_ref, vb_ref, qi_ref, kid_ref, wi_ref):
    x = x_ref[...]
    ms = jnp.mean(x * x, axis=-1, keepdims=True)
    h = (x * lax.rsqrt(ms + EPS) * g1_ref[...]).astype(MXU_DTYPE)

    def proj(j):
        return _dot(h, w_ref[:, j * WIDTH:(j + 1) * WIDTH])

    def head_norm(p, g_ref):
        ss = _dot(_split_hi_lo(p * p), seg_ref[...])
        return p * lax.rsqrt(ss * (1.0 / HEAD_DIM) + EPS) * g_ref[...]

    scale = HEAD_DIM ** -0.5
    qa_ref[...] = (proj(0) * scale).astype(MXU_DTYPE)
    p = proj(1)
    ak_ref[...] = p
    ka_ref[...] = p.astype(MXU_DTYPE)
    p = proj(2)
    av_ref[...] = p
    va_ref[...] = p.astype(MXU_DTYPE)
    qb_ref[...] = (head_norm(proj(3), gq_ref) * (scale * LOG2E)).astype(MXU_DTYPE)
    p = head_norm(proj(4), gk_ref)
    bk_ref[...] = p
    kb_ref[...] = p.astype(MXU_DTYPE)
    p = proj(5)
    bv_ref[...] = p
    vb_ref[...] = p.astype(MXU_DTYPE)
    qi_ref[...] = proj(6).astype(MXU_DTYPE)
    t = _dot(h, wt_ref[...])
    ik_ref[...] = t[:, :IDX_DIM]
    kid_ref[...] = t[:, :LANES].astype(MXU_DTYPE)
    wi_ref[...] = t[:, LANES:] * (IDX_DIM ** -0.5 * N_HEADS ** -0.5)


def _in_proj(x2d, g1, w_main, w_tail, gq, gk, seg):
    rows, d = x2d.shape
    tm = min(ROW_TILE, rows)
    assert rows % tm == 0
    row_spec = lambda w: pl.BlockSpec((tm, w), lambda i: (i, 0))
    full = lambda a: pl.BlockSpec(a.shape, lambda i: (0,) * a.ndim)
    f32_out = lambda w: jax.ShapeDtypeStruct((rows, w), F32)
    mx_out = lambda w: jax.ShapeDtypeStruct((rows, w), MXU_DTYPE)
    return pl.pallas_call(
        _in_proj_kernel,
        grid=(rows // tm,),
        in_specs=[row_spec(d), full(g1), full(w_main), full(w_tail), full(gq), full(gk), full(seg)],
        out_specs=[row_spec(WIDTH)] * 4 + [row_spec(IDX_DIM)] + [row_spec(WIDTH)] * 7
                  + [row_spec(LANES), row_spec(LANES)],
        out_shape=[f32_out(WIDTH)] * 4 + [f32_out(IDX_DIM)] + [mx_out(WIDTH)] * 7
                  + [mx_out(LANES), f32_out(LANES)],
        compiler_params=_params(("parallel",)),
        name="in_proj",
    )(x2d, g1, w_main, w_tail, gq, gk, seg)


def _head_block_diag(x, first_half):
    return jnp.concatenate([jnp.where(first_half, x, 0), jnp.where(first_half, 0, x)], axis=0)


def _stick_breaking_kernel(q_ref, k_ref, v_ref, tri_ref, o_ref, lb_ref, hl_ref, w_ref, *, q_off, tq):
    p0 = q_off + pl.program_id(2) * tq
    q = q_ref[...]
    lane = lax.broadcasted_iota(I32, (1, LANES), 1)
    lane2 = lax.broadcasted_iota(I32, (1, 2 * LANES), 1)
    first_half = lane < HEAD_DIM
    key_off = lane2 & (LANES - 1)
    row_pos = p0 + lax.broadcasted_iota(I32, (tq, 1), 0)
    n_blocks = (p0 + tq - 1 + LANES - 1) // LANES
    n_diag = n_blocks - p0 // LANES
    tri = tri_ref[...]

    def block_start(j):
        return pl.multiple_of(jnp.clip(n_blocks - 1 - j, 0, n_blocks - 1) * LANES, LANES)

    row_chunks = [slice(r, r + min(tq, SB_ROW_CHUNK)) for r in range(0, tq, min(tq, SB_ROW_CHUNK))]
    one = jnp.minimum(n_blocks, 1).astype(F32)

    def half_step(masked, j, slot, carry):
        acc, later0, later1, later0_p, later1_p = carry
        other = 1 - slot
        start = block_start(j)
        k_bd = _head_block_diag(k_ref[pl.ds(start, LANES), :], first_half)
        z = _dot_nt(q, k_bd)
        if masked:
            key_pos = jnp.where(j < n_blocks, start, q_off + tq * pl.num_programs(2)) + key_off
        sums = ([], [])
        for rows in row_chunks:
            zc = z[rows, :]
            log_beta = jnp.minimum(zc, 0.0) - jnp.log(one + jnp.exp(-jnp.abs(zc)))
            log_1mb = log_beta - zc
            if masked:
                causal = key_pos < row_pos[rows, :]
                log_1mb = jnp.where(causal, log_1mb, 0.0)
                log_beta = jnp.where(causal, log_beta, NEG)
            hi = log_1mb.astype(MXU_DTYPE)
            lb_ref[slot, rows, :] = log_beta
            hl_ref[slot, rows, :2 * LANES] = hi
            hl_ref[slot, rows, 2 * LANES:] = (log_1mb - hi.astype(F32)).astype(MXU_DTYPE)
            sums[0].append(jnp.sum(log_1mb[:, :LANES], axis=1, keepdims=True))
            sums[1].append(jnp.sum(log_1mb[:, LANES:], axis=1, keepdims=True))
        new_later0 = later0 + jnp.concatenate(sums[0], axis=0)
        new_later1 = later1 + jnp.concatenate(sums[1], axis=0)
        w_pp = w_ref[slot]
        within = _dot(hl_ref[other], tri)
        for rows in row_chunks:
            n = rows.stop - rows.start
            later = jnp.concatenate([jnp.broadcast_to(later0_p[rows, :], (n, LANES)),
                                     jnp.broadcast_to(later1_p[rows, :], (n, LANES))], axis=1)
            w_ref[other, rows, :] = jnp.exp(lb_ref[other, rows, :] + (within[rows, :] + later)).astype(MXU_DTYPE)
        v_bd = _head_block_diag(v_ref[pl.ds(block_start(j - 2), LANES), :], first_half)
        acc = acc + _dot(w_pp, v_bd)
        return acc, new_later0, new_later1, later0, later1

    def pair_step(masked, p, carry):
        carry = half_step(masked, 2 * p, 0, carry)
        return half_step(masked, 2 * p + 1, 1, carry)

    lb_ref[...] = jnp.full(lb_ref.shape, NEG, F32)
    hl_ref[...] = jnp.zeros(hl_ref.shape, MXU_DTYPE)
    w_ref[...] = jnp.zeros(w_ref.shape, MXU_DTYPE)
    zero_col = row_pos.astype(F32) * 0.0
    carry = (jnp.zeros((tq, LANES), F32), zero_col, zero_col, zero_col, zero_col)
    pairs_diag = (n_diag + 1) // 2
    pairs_full = jnp.maximum(n_blocks // 2, pairs_diag)
    pairs_all = (n_blocks + 2 + 1) // 2
    def weights_vanished(carry):
        return (jnp.maximum(jnp.max(carry[1]), jnp.max(carry[2])) < SB_VANISH_LOG).astype(I32)

    def unmasked_pair(state):
        p, _, carry = state
        carry = pair_step(False, p, carry)
        return p + 1, weights_vanished(carry), carry

    carry = lax.fori_loop(0, pairs_diag, functools.partial(pair_step, True), carry)
    p, vanished, carry = lax.while_loop(lambda s: (s[0] < pairs_full) & (s[1] == 0), unmasked_pair,
                                        (pairs_diag, weights_vanished(carry), carry))
    carry = lax.fori_loop(p, jnp.where(vanished == 1, p + 1, pairs_all), functools.partial(pair_step, True), carry)
    o_ref[...] = carry[0].astype(o_ref.dtype)


def _stick_breaking(q, k, v, tri, q_off):
    b, t, _ = q.shape
    lp = k.shape[1]
    tq = min(SB_Q_TILE, t)
    assert t % tq == 0 and lp % SB_KEY_TILE == 0
    pairs = WIDTH // LANES
    kv_spec = pl.BlockSpec((None, lp, LANES), lambda bi, pi, qi: (bi, 0, pi))
    q_spec = pl.BlockSpec((None, tq, LANES), lambda bi, pi, qi: (bi, qi, pi))
    return pl.pallas_call(
        functools.partial(_stick_breaking_kernel, q_off=q_off, tq=tq),
        grid=(b, pairs, t // tq),
        in_specs=[q_spec, kv_spec, kv_spec, pl.BlockSpec(tri.shape, lambda bi, pi, qi: (0, 0))],
        out_specs=q_spec,
        out_shape=jax.ShapeDtypeStruct(q.shape, MXU_DTYPE),
        scratch_shapes=[pltpu.VMEM((2, tq, 2 * LANES), F32),
                        pltpu.VMEM((2, tq, 4 * LANES), MXU_DTYPE),
                        pltpu.VMEM((2, tq, 2 * LANES), MXU_DTYPE)],
        compiler_params=_params(("parallel", "parallel", "arbitrary")),
        name="stick_breaking",
    )(q, k, v, tri)


def _bias_tiles_kernel(rb_ref, bucket_ref, tb_ref):
    h = pl.program_id(0)
    tiles = []
    for d in range(bucket_ref.shape[0]):
        bucket = bucket_ref[d]
        acc = jnp.zeros(bucket.shape, F32)
        for j in range(N_BUCKETS):
            acc = jnp.where(bucket == j, rb_ref[j, h], acc)
        tiles.append(acc)
    for d, tile in enumerate(tiles):
        tb_ref[d] = (tile - tiles[-1]) * LOG2E


def _t5_bucket(rel):
    half = N_BUCKETS // 2
    max_exact = half // 2
    base = jnp.where(rel > 0, half, 0)
    n = jnp.abs(rel)
    large = max_exact + (jnp.log(jnp.maximum(n, 1).astype(jnp.float32) / max_exact)
                         / math.log(T5_MAX_DISTANCE / max_exact) * (half - max_exact)).astype(jnp.int32)
    large = jnp.minimum(large, half - 1)
    return base + jnp.where(n < max_exact, n, large)


N_BIAS_TILES = 3
N_NEAR_TILES = 2


def _bias_tiles(rel_bias):
    assert T5_MAX_DISTANCE <= LANES
    j = jnp.arange(LANES, dtype=I32)[None, :, None]
    i = jnp.arange(LANES, dtype=I32)[None, None, :]
    d = jnp.arange(N_BIAS_TILES, dtype=I32)[:, None, None]
    bucket = _t5_bucket(j - i - LANES * d).astype(I32)
    return pl.pallas_call(
        _bias_tiles_kernel,
        grid=(N_HEADS,),
        in_specs=[pl.BlockSpec(memory_space=pltpu.SMEM),
                  pl.BlockSpec(bucket.shape, lambda h: (0, 0, 0))],
        out_specs=pl.BlockSpec((N_BIAS_TILES, LANES, LANES), lambda h: (h, 0, 0)),
        out_shape=jax.ShapeDtypeStruct((N_HEADS * N_BIAS_TILES, LANES, LANES), F32),
        compiler_params=_params(("arbitrary",)),
        name="bias_tiles",
    )(rel_bias, bucket)


def _sparse_kernel(q_ref, k_ref, vt_ref, qi_ref, kid_ref, wit_ref, tb_ref, ot_ref,
                   key_ref, mb_ref, cut_ref, *, q_off, tq, n_keys, top_k):
    kt_w = KEY_TILE
    p0 = q_off + pl.program_id(1) * tq
    lane = lax.broadcasted_iota(I32, (1, LANES), 1)
    key_row = lax.broadcasted_iota(I32, (kt_w, tq), 0)
    q_chunk = lax.shift_right_arithmetic(p0 + lax.broadcasted_iota(I32, (1, tq), 1), CHUNK_SHIFT)
    adm_end = (lax.shift_right_arithmetic(p0 + tq - 1, CHUNK_SHIFT) + 1) * CHUNK
    n_blocks = (jnp.minimum(adm_end, n_keys) + kt_w - 1) // kt_w
    half_masks = [lane < HEAD_DIM, lane >= HEAD_DIM]

    def admissible(start):
        pos = start + key_row
        return (lax.shift_right_arithmetic(pos, CHUNK_SHIFT) <= q_chunk) & (pos < n_keys)

    wit = wit_ref[...]
    qi_heads = []
    for h in range(N_HEADS):
        pair = qi_ref[:, (h // 2) * LANES:(h // 2 + 1) * LANES]
        qi_heads.append(jnp.where(half_masks[h % 2], pair, 0))

    def score_block(kb, _):
        start = pl.multiple_of(kb * kt_w, kt_w)
        kt = kid_ref[pl.ds(start, kt_w), :]
        acc = jnp.zeros((kt_w, tq), F32)
        for h in range(N_HEADS):
            acc = acc + jnp.maximum(_dot_nt(kt, qi_heads[h]), 0.0) * wit[h:h + 1, :]
        acc = jnp.where(acc == 0.0, 0.0, acc)
        bits = lax.bitcast_convert_type(acc, I32)
        key = jnp.where(bits < 0, bits ^ 0x7FFFFFFF, bits)
        key_ref[kb] = jnp.where(admissible(start), key, INT_MIN)
        return 0

    lax.fori_loop(0, n_blocks, score_block, 0)

    def count(pred):
        def blk(kb, cnt):
            for r in range(0, kt_w, COUNT_ROWS):
                hit = jnp.where(pred(key_ref[kb, r:r + COUNT_ROWS, :], kb * kt_w + r), 1.0, 0.0)
                parts = [hit[i * SUBLANES:(i + 1) * SUBLANES, :] for i in range(COUNT_ROWS // SUBLANES)]
                while len(parts) > 1:
                    parts = [a + b for a, b in zip(parts[::2], parts[1::2])]
                cnt = cnt + parts[0]
            return cnt
        cnt = lax.fori_loop(0, n_blocks, blk, jnp.zeros((SUBLANES, tq), F32))
        return jnp.sum(cnt, axis=0, keepdims=True)

    k_f = float(top_k)
    thr = jnp.where(count(lambda key, row0: key >= 0) >= k_f, 0, INT_MIN).astype(I32)

    def bit_step(i, thr):
        cand = thr | lax.shift_left(jnp.int32(1), 30 - i)
        return jnp.where(count(lambda key, row0: key >= cand) >= k_f, cand, thr)

    thr = lax.fori_loop(0, 31, bit_step, thr)

    n_ge = count(lambda key, row0: key >= thr)
    n_cols = key_ref.shape[0] * kt_w
    cut_ref[...] = jnp.full(cut_ref.shape, n_cols, I32)

    @pl.when(jnp.max(n_ge) > k_f)
    def _():
        need = k_f - count(lambda key, row0: key > thr)
        chunk_row = lax.broadcasted_iota(I32, (COUNT_ROWS, tq), 0)

        def cut_step(i, cut):
            cand = cut | lax.shift_left(jnp.int32(1), n_cols.bit_length() - 1 - i)
            n_eq = count(lambda key, row0: (key == thr) & ((row0 + chunk_row) < cand))
            return jnp.where(n_eq <= need, cand, cut)

        cut = lax.fori_loop(0, n_cols.bit_length(), cut_step, jnp.zeros((1, tq), I32))
        cut_ref[...] = jnp.broadcast_to(cut, cut_ref.shape)

    cut = cut_ref[0:1, :]

    def mask_block(kb, _):
        start = kb * kt_w
        key = key_ref[kb]
        sel = (key > thr) | ((key == thr) & ((start + key_row) < cut))
        mb_ref[kb] = jnp.where(sel & admissible(start), 0.0, NEG)
        return 0

    lax.fori_loop(0, n_blocks, mask_block, 0)

    n_pairs = WIDTH // LANES
    first_half = half_masks[0]
    top_rows = lax.broadcasted_iota(I32, (LANES, 1), 0) < HEAD_DIM

    def attend(near, kb, carry):
        start = pl.multiple_of(kb * kt_w, kt_w)
        mb = mb_ref[kb]
        q_groups = [(c, min(LANES, tq - c)) for c in range(0, tq, LANES)]
        tile_d = [[jnp.clip((p0 + c - (start + sub * LANES)) // LANES, 0, N_BIAS_TILES - 1) for c, _ in q_groups]
                  for sub in range(kt_w // LANES)]
        def pair_scores(pair):
            lanes = slice(pair * LANES, (pair + 1) * LANES)
            k_bd = _head_block_diag(k_ref[pl.ds(start, kt_w), lanes], first_half)
            return _dot_nt(k_bd, q_ref[:, lanes])

        scores = {pair: pair_scores(pair) for pair in range(n_pairs)}
        out = []
        for pair in range(n_pairs):
            lanes = slice(pair * LANES, (pair + 1) * LANES)
            m, l, acc = carry[pair]
            vt = vt_ref[lanes, pl.ds(start, kt_w)]
            vt_bd = jnp.concatenate([jnp.where(top_rows, vt, 0), jnp.where(top_rows, 0, vt)], axis=1)
            z = scores.pop(pair)
            m_new, alpha, p = [], [], []
            for hh in range(2):
                head = 2 * pair + hh
                s = z[hh * kt_w:(hh + 1) * kt_w, :] + mb
                if near:
                    bias = [jnp.concatenate([tb_ref[head * N_BIAS_TILES + d][:, :w] for d, (_, w) in zip(ds, q_groups)],
                                            axis=1) for ds in tile_d]
                    s = s + jnp.concatenate(bias, axis=0)
                m_h = jnp.maximum(m[hh], jnp.max(s, axis=0, keepdims=True))
                a_h = jnp.exp2(m[hh] - m_h)
                p_h = jnp.exp2(s - m_h)
                l = l[:hh] + (a_h * l[hh] + jnp.sum(p_h, axis=0, keepdims=True),) + l[hh + 1:]
                m_new.append(m_h)
                alpha.append(a_h)
                p.append(p_h.astype(MXU_DTYPE))
            acc = jnp.where(top_rows, alpha[0], alpha[1]) * acc + _dot(vt_bd, jnp.concatenate(p, axis=0))
            out.append((tuple(m_new), l, acc))
        return tuple(out)

    neg_row = jnp.full((1, tq), NEG, F32)
    zero_row = jnp.zeros((1, tq), F32)
    init = tuple(((neg_row, neg_row), (zero_row, zero_row), jnp.zeros((LANES, tq), F32)) for _ in range(n_pairs))
    n_far = jnp.maximum(n_blocks - N_NEAR_TILES, 0)
    carry = lax.fori_loop(0, n_far, functools.partial(attend, False), init)
    final = lax.fori_loop(n_far, n_blocks, functools.partial(attend, True), carry)
    for pair in range(n_pairs):
        _, l, acc = final[pair]
        ot_ref[pair * LANES:(pair + 1) * LANES, :] = (acc / jnp.where(top_rows, l[0], l[1])).astype(ot_ref.dtype)


def _sparse_attention(q, k, v, qi, kid, wi, tiles, q_off, n_keys):
    b, t, _ = q.shape
    lp = k.shape[1]
    tq = min(Q_TILE, t)
    assert t % tq == 0 and lp % KEY_TILE == 0 and q_off % LANES == 0 and (tq % LANES == 0 or t == tq)
    assert LANES + tq <= KEY_TILE
    top_k = min(TOPK_MAX, n_keys // 4)
    n_blocks = lp // KEY_TILE
    vt = v.transpose(0, 2, 1)
    wit = wi[:, :, :N_HEADS].transpose(0, 2, 1)
    q_spec = lambda w: pl.BlockSpec((None, tq, w), lambda bi, qi_: (bi, qi_, 0))
    k_spec = lambda w: pl.BlockSpec((None, lp, w), lambda bi, qi_: (bi, 0, 0))
    qt_spec = lambda rows: pl.BlockSpec((None, rows, tq), lambda bi, qi_: (bi, 0, qi_))
    out_t = pl.pallas_call(
        functools.partial(_sparse_kernel, q_off=q_off, tq=tq, n_keys=n_keys, top_k=top_k),
        grid=(b, t // tq),
        in_specs=[q_spec(WIDTH), k_spec(WIDTH), pl.BlockSpec((None, WIDTH, lp), lambda bi, qi_: (bi, 0, 0)),
                  q_spec(WIDTH), k_spec(LANES), qt_spec(N_HEADS),
                  pl.BlockSpec(tiles.shape, lambda bi, qi_: (0, 0, 0))],
        out_specs=qt_spec(WIDTH),
        out_shape=jax.ShapeDtypeStruct((b, WIDTH, t), MXU_DTYPE),
        scratch_shapes=[pltpu.VMEM((n_blocks, KEY_TILE, tq), I32),
                        pltpu.VMEM((n_blocks, KEY_TILE, tq), F32),
                        pltpu.VMEM((SUBLANES, tq), I32)],
        compiler_params=_params(("parallel", "arbitrary")),
        name="sparse_attention",
    )(q, k, vt, qi, kid, wit, tiles)
    return out_t.transpose(0, 2, 1)


def _ffn_kernel(x_ref, oa_ref, ob_ref, wo_ref, g2_ref, wup_ref, cw_ref, cb_ref, wdn_ref, st_ref,
                y_ref, cs_ref, x2_ref, h2_ref, acc_ref, prev_ref, u_ref, *, tm, n_chunks):
    keep = CONV_WIDTH - 1

    @pl.when(pl.program_id(1) == 0)
    def _():
        prev_ref[:, :, SUBLANES - keep:, :] = st_ref[...]

    mix = jnp.concatenate([oa_ref[...], ob_ref[...]], axis=1)
    x2 = x_ref[...] + _dot(mix, wo_ref[...])
    x2_ref[...] = x2
    ms = jnp.mean(x2 * x2, axis=-1, keepdims=True)
    h2_ref[...] = (x2 * lax.rsqrt(ms + EPS) * g2_ref[...]).astype(MXU_DTYPE)
    acc_ref[...] = jnp.zeros(acc_ref.shape, F32)
    row = lax.broadcasted_iota(I32, (tm, 1), 0)

    def up_proj(c, slot):
        h2 = h2_ref[...]
        for half in range(2):
            u_ref[slot, half] = _dot(h2, wup_ref[half, c])

    def finish_chunk(c, slot):
        conv = []
        for half in range(2):
            u = u_ref[slot, half]
            prev = prev_ref[half, c]
            before2, before1 = prev[SUBLANES - 2:SUBLANES - 1, :], prev[SUBLANES - 1:, :]
            u1 = jnp.where(row == 0, before1, pltpu.roll(u, 1, 0))
            u2 = jnp.where(row == 0, before2, jnp.where(row == 1, before1, pltpu.roll(u, 2, 0)))
            w = cw_ref[half, c]
            conv.append(cb_ref[half, c] + u2 * w[0:1, :] + u1 * w[1:2, :] + u * w[2:3, :])
            prev_ref[half, c] = u[tm - SUBLANES:, :]
            cs_ref[half, c] = u[tm - keep:, :]
        a, g = conv
        act = g * (1.0 / (1.0 + jnp.exp(-g))) * a
        acc_ref[...] += _dot(act.astype(MXU_DTYPE), wdn_ref[c])

    def chunk_pair(p, _):
        c = 2 * p
        up_proj(c + 1, 1)
        finish_chunk(c, 0)
        up_proj(c + 2, 0)
        finish_chunk(c + 1, 1)
        return 0

    up_proj(0, 0)
    n_pairs = (n_chunks - 1) // 2
    lax.fori_loop(0, n_pairs, chunk_pair, 0)
    last = 2 * n_pairs
    if last + 1 < n_chunks:
        up_proj(last + 1, 1)
        finish_chunk(last, 0)
        finish_chunk(last + 1, 1)
    else:
        finish_chunk(last, 0)
    y_ref[...] = x2_ref[...] + acc_ref[...]


def _ffn(x, oa, ob, w_out, g2, w_up, conv_w, conv_b, w_down, state):
    b, t, d = x.shape
    tm = min(ROW_TILE, t)
    assert t % tm == 0 and tm >= SUBLANES and CONV_WIDTH == 3
    _, n_chunks, _, fc = w_up.shape
    full = lambda a: pl.BlockSpec(a.shape, lambda bi, ti: (0,) * a.ndim)
    row_spec = lambda w: pl.BlockSpec((None, tm, w), lambda bi, ti: (bi, ti, 0))
    st_spec = pl.BlockSpec((None,) + state.shape[1:], lambda bi, ti: (bi, 0, 0, 0, 0))
    return pl.pallas_call(
        functools.partial(_ffn_kernel, tm=tm, n_chunks=n_chunks),
        grid=(b, t // tm),
        in_specs=[row_spec(d), row_spec(WIDTH), row_spec(WIDTH), full(w_out), full(g2), full(w_up),
                  full(conv_w), full(conv_b), full(w_down), st_spec],
        out_specs=[row_spec(d), st_spec],
        out_shape=[jax.ShapeDtypeStruct(x.shape, F32), jax.ShapeDtypeStruct(state.shape, F32)],
        scratch_shapes=[pltpu.VMEM((tm, d), F32), pltpu.VMEM((tm, d), MXU_DTYPE), pltpu.VMEM((tm, d), F32),
                        pltpu.VMEM((2, n_chunks, SUBLANES, fc), F32),
                        pltpu.VMEM((2, 2, tm, fc), F32)],
        compiler_params=_params(("parallel", "arbitrary")),
        name="ffn",
    )(x, oa, ob, w_out, g2, w_up, conv_w, conv_b, w_down, state)


def _pad_keys(a, lp):
    return jnp.pad(a, ((0, 0), (0, lp - a.shape[1]), (0, 0)))


def _state_to_chunks(s, fc):
    b, keep, f2 = s.shape
    return s.reshape(b, keep, 2, f2 // 2 // fc, fc).transpose(0, 2, 3, 1, 4)


def _state_from_chunks(s):
    b, _, n_chunks, keep, fc = s.shape
    return s.transpose(0, 3, 1, 2, 4).reshape(b, keep, 2 * n_chunks * fc)


def _layer(x, caches, lw, consts):
    b, t, d = x.shape
    tri, seg, tiles = consts
    outs = _in_proj(x.reshape(b * t, d), lw["g1"], lw["w_main"], lw["w_tail"], lw["gq"], lw["gk"], seg)
    ak, av, bk, bv, ik = [o.reshape(b, t, -1) for o in outs[:5]]
    qa, ka, va, qb, kb, vb, qi, kid, wi = [o.reshape(b, t, -1) for o in outs[5:]]
    fc = lw["w_up"].shape[-1]
    if caches is None:
        past = 0
        state = jnp.zeros((b, 2, lw["w_up"].shape[1], CONV_WIDTH - 1, fc), F32)
    else:
        c_ak, c_av, c_bk, c_bv, c_ik, c_conv = caches
        past = c_ak.shape[1]
        flat = lambda c: c.reshape(b, past, -1).astype(MXU_DTYPE)
        ka = jnp.concatenate([flat(c_ak), ka], axis=1)
        va = jnp.concatenate([flat(c_av), va], axis=1)
        kb = jnp.concatenate([flat(c_bk), kb], axis=1)
        vb = jnp.concatenate([flat(c_bv), vb], axis=1)
        kid = jnp.concatenate([jnp.concatenate([flat(c_ik)] * 2, axis=-1), kid], axis=1)
        state = _state_to_chunks(c_conv, fc)
    n_keys = past + t
    lp = -(-n_keys // KEY_TILE) * KEY_TILE
    ka, va, kb, vb, kid = [_pad_keys(a, lp) for a in (ka, va, kb, vb, kid)]
    oa = _stick_breaking(qa, ka, va, tri, past)
    ob = _sparse_attention(qb, kb, vb, qi, kid, wi, tiles, past, n_keys)
    y, new_state = _ffn(x, oa, ob, lw["w_out"], lw["g2"], lw["w_up"], lw["conv_w"], lw["conv_b"],
                        lw["w_down"], state)
    heads = lambda a: a.reshape(b, t, N_HEADS, HEAD_DIM)
    return y, (heads(ak), heads(av), heads(bk), heads(bv), ik, _state_from_chunks(new_state))


def _layer_weights(l, norm1, w_in, q_norm, k_norm, w_out, norm2, w_up, conv_w, conv_b, w_down):
    d = w_in.shape[1]
    f = w_down.shape[1]
    fc = FF_TILE if f % FF_TILE == 0 else LANES
    assert f % fc == 0 and w_in.shape[2] == 7 * WIDTH + IDX_DIM + N_HEADS
    n_chunks = f // fc
    w = w_in[l]
    main = 7 * WIDTH
    w_ik, w_iw = w[:, main:main + IDX_DIM], w[:, main + IDX_DIM:]
    w_tail = jnp.concatenate([w_ik, w_ik, w_iw, jnp.zeros((d, LANES - N_HEADS), w.dtype)], axis=1)
    return {
        "g1": norm1[l][None, :],
        "w_main": w[:, :main].astype(MXU_DTYPE),
        "w_tail": w_tail.astype(MXU_DTYPE),
        "gq": jnp.tile(q_norm[l], N_HEADS)[None, :],
        "gk": jnp.tile(k_norm[l], N_HEADS)[None, :],
        "w_out": w_out[l].astype(MXU_DTYPE),
        "g2": norm2[l][None, :],
        "w_up": w_up[l].astype(MXU_DTYPE).reshape(d, 2, n_chunks, fc).transpose(1, 2, 0, 3),
        "conv_w": conv_w[l].reshape(CONV_WIDTH, 2, n_chunks, fc).transpose(1, 2, 0, 3),
        "conv_b": conv_b[l].reshape(2, n_chunks, 1, fc),
        "w_down": w_down[l].astype(MXU_DTYPE).reshape(n_chunks, fc, d),
    }


def kernel(x_prompt, x_sample, cache_a_k, cache_a_v, cache_b_k, cache_b_v, cache_idx_k, state_ffn_conv,
           rel_bias, norm1, w_in, q_norm, k_norm, w_out, norm2, w_up, conv_w, conv_b, w_down):
    depth = w_in.shape[0]
    assert cache_a_k.shape[3:] == (N_HEADS, HEAD_DIM) and cache_idx_k.shape[-1] == IDX_DIM
    r2 = jnp.arange(2 * LANES, dtype=I32)
    tri = ((r2[:, None] > r2[None, :]) & (r2[:, None] // LANES == r2[None, :] // LANES)).astype(MXU_DTYPE)
    tri = jnp.concatenate([tri, tri], axis=0)
    g = jnp.arange(WIDTH, dtype=I32) // HEAD_DIM
    seg = (g[:, None] == g[None, :]).astype(MXU_DTYPE)
    seg = jnp.concatenate([seg, seg], axis=0)
    consts = (tri, seg, _bias_tiles(rel_bias))

    yp, ys = x_prompt, x_sample
    p_states, s_states = [], []
    for l in range(depth):
        lw = _layer_weights(l, norm1, w_in, q_norm, k_norm, w_out, norm2, w_up, conv_w, conv_b, w_down)
        yp, st_p = _layer(yp, None, lw, consts)
        caches = (cache_a_k[l], cache_a_v[l], cache_b_k[l], cache_b_v[l], cache_idx_k[l], state_ffn_conv[l])
        ys, st_s = _layer(ys, caches, lw, consts)
        p_states.append(st_p)
        s_states.append(st_s)
    stack = lambda sts, i: jnp.stack([s[i] for s in sts], axis=0)
    return (yp, ys) + tuple(stack(p_states, i) for i in range(6)) + tuple(stack(s_states, i) for i in range(6))
```

```python
import functools
import math

import jax
import jax.numpy as jnp
from jax import lax
from jax.experimental import pallas as pl
from jax.experimental.pallas import tpu as pltpu

HEAD_DIM = 64
N_HEADS = 8
IDX_DIM = 64
WIDTH = N_HEADS * HEAD_DIM
CHUNK = 64
CHUNK_SHIFT = 6
TOPK_MAX = 256
N_BUCKETS = 32
T5_MAX_DISTANCE = 128
CONV_WIDTH = 3
EPS = 1e-6

LANES = 128
SUBLANES = 8
Q_TILE = 256
COUNT_ROWS = 64
SB_Q_TILE = 256
SB_KEY_TILE = 256
SB_ROW_CHUNK = 32
SB_VANISH_LOG = -105.0
KEY_TILE = 512
ROW_TILE = 256
FF_TILE = 256
CACHE_TILE = 256
VMEM_LIMIT_BYTES = 56 * 1024 * 1024

MXU_DTYPE = jnp.bfloat16
F32 = jnp.float32
I32 = jnp.int32
INT_MIN = -(2 ** 31)
NEG = -1e30
LOG2E = 1.0 / math.log(2.0)

_NT = (((1,), (1,)), ((), ()))


def _dot(a, b):
    return jnp.dot(a, b, preferred_element_type=F32)


def _dot_nt(a, b):
    return lax.dot_general(a, b, _NT, preferred_element_type=F32)


def _split_hi_lo(x):
    hi = x.astype(MXU_DTYPE)
    lo = (x - hi.astype(F32)).astype(MXU_DTYPE)
    return jnp.concatenate([hi, lo], axis=1)


def _params(semantics):
    return pltpu.CompilerParams(dimension_semantics=semantics, vmem_limit_bytes=VMEM_LIMIT_BYTES)


def _in_proj_kernel(x_ref, g1_ref, w_ref, wt_ref, gq_ref, gk_ref, seg_ref,
                    ak_ref, av_ref, bk_ref, bv_ref, ik_ref,
                    qa_ref, ka_ref, va_ref, qb_ref, kb_ref, vb_ref, qi_ref, kid_ref, wi_ref):
    x = x_ref[...]
    ms = jnp.mean(x * x, axis=-1, keepdims=True)
    h = (x * lax.rsqrt(ms + EPS) * g1_ref[...]).astype(MXU_DTYPE)

    def proj(j):
        return _dot(h, w_ref[:, j * WIDTH:(j + 1) * WIDTH])

    def head_norm(p, g_ref):
        ss = _dot(_split_hi_lo(p * p), seg_ref[...])
        return p * lax.rsqrt(ss * (1.0 / HEAD_DIM) + EPS) * g_ref[...]

    scale = HEAD_DIM ** -0.5
    qa_ref[...] = (proj(0) * scale).astype(MXU_DTYPE)
    p = proj(1)
    ak_ref[...] = p
    ka_ref[...] = p.astype(MXU_DTYPE)
    p = proj(2)
    av_ref[...] = p
    va_ref[...] = p.astype(MXU_DTYPE)
    qb_ref[...] = (head_norm(proj(3), gq_ref) * (scale * LOG2E)).astype(MXU_DTYPE)
    p = head_norm(proj(4), gk_ref)
    bk_ref[...] = p
    kb_ref[...] = p.astype(MXU_DTYPE)
    p = proj(5)
    bv_ref[...] = p
    vb_ref[...] = p.astype(MXU_DTYPE)
    qi_ref[...] = proj(6).astype(MXU_DTYPE)
    t = _dot(h, wt_ref[...])
    ik_ref[...] = t[:, :IDX_DIM]
    kid_ref[...] = t[:, :LANES].astype(MXU_DTYPE)
    wi_ref[...] = t[:, LANES:] * (IDX_DIM ** -0.5 * N_HEADS ** -0.5)


def _in_proj(x2d, g1, w_main, w_tail, gq, gk, seg):
    rows, d = x2d.shape
    tm = min(ROW_TILE, rows)
    assert rows % tm == 0
    row_spec = lambda w: pl.BlockSpec((tm, w), lambda i: (i, 0))
    full = lambda a: pl.BlockSpec(a.shape, lambda i: (0,) * a.ndim)
    f32_out = lambda w: jax.ShapeDtypeStruct((rows, w), F32)
    mx_out = lambda w: jax.ShapeDtypeStruct((rows, w), MXU_DTYPE)
    return pl.pallas_call(
        _in_proj_kernel,
        grid=(rows // tm,),
        in_specs=[row_spec(d), full(g1), full(w_main), full(w_tail), full(gq), full(gk), full(seg)],
        out_specs=[row_spec(WIDTH)] * 4 + [row_spec(IDX_DIM)] + [row_spec(WIDTH)] * 7
                  + [row_spec(LANES), row_spec(LANES)],
        out_shape=[f32_out(WIDTH)] * 4 + [f32_out(IDX_DIM)] + [mx_out(WIDTH)] * 7
                  + [mx_out(LANES), f32_out(LANES)],
        compiler_params=_params(("parallel",)),
        name="in_proj",
    )(x2d, g1, w_main, w_tail, gq, gk, seg)


def _head_block_diag(x, first_half):
    return jnp.concatenate([jnp.where(first_half, x, 0), jnp.where(first_half, 0, x)], axis=0)


def _stick_breaking_kernel(q_ref, k_ref, v_ref, tri_ref, o_ref, lb_ref, hl_ref, w_ref, *, q_off, tq):
    p0 = q_off + pl.program_id(2) * tq
    q = q_ref[...]
    lane = lax.broadcasted_iota(I32, (1, LANES), 1)
    lane2 = lax.broadcasted_iota(I32, (1, 2 * LANES), 1)
    first_half = lane < HEAD_DIM
    key_off = lane2 & (LANES - 1)
    row_pos = p0 + lax.broadcasted_iota(I32, (tq, 1), 0)
    n_blocks = (p0 + tq - 1 + LANES - 1) // LANES
    n_diag = n_blocks - p0 // LANES
    tri = tri_ref[...]

    def block_start(j):
        return pl.multiple_of(jnp.clip(n_blocks - 1 - j, 0, n_blocks - 1) * LANES, LANES)

    row_chunks = [slice(r, r + min(tq, SB_ROW_CHUNK)) for r in range(0, tq, min(tq, SB_ROW_CHUNK))]
    one = jnp.minimum(n_blocks, 1).astype(F32)

    def half_step(masked, j, slot, carry, stages=(1, 2, 3)):
        acc, later0, later1, later0_p, later1_p = carry
        other = 1 - slot
        if 1 not in stages:
            if 2 in stages:
                weights_stage(other, later0_p, later1_p)
            return accumulate_stage(j, slot, acc), later0, later1, later0, later1
        start = block_start(j)
        k_bd = _head_block_diag(k_ref[pl.ds(start, LANES), :], first_half)
        z = _dot_nt(q, k_bd)
        if masked:
            key_pos = jnp.where(j < n_blocks, start, q_off + tq * pl.num_programs(2)) + key_off
        sums = ([], [])
        for rows in row_chunks:
            zc = z[rows, :]
            log_beta = jnp.minimum(zc, 0.0) - jnp.log(one + jnp.exp(-jnp.abs(zc)))
            log_1mb = log_beta - zc
            if masked:
                causal = key_pos < row_pos[rows, :]
                log_1mb = jnp.where(causal, log_1mb, 0.0)
                log_beta = jnp.where(causal, log_beta, NEG)
            hi = log_1mb.astype(MXU_DTYPE)
            lb_ref[slot, rows, :] = log_beta
            hl_ref[slot, rows, :2 * LANES] = hi
            hl_ref[slot, rows, 2 * LANES:] = (log_1mb - hi.astype(F32)).astype(MXU_DTYPE)
            sums[0].append(jnp.sum(log_1mb[:, :LANES], axis=1, keepdims=True))
            sums[1].append(jnp.sum(log_1mb[:, LANES:], axis=1, keepdims=True))
        new_later0 = later0 + jnp.concatenate(sums[0], axis=0)
        new_later1 = later1 + jnp.concatenate(sums[1], axis=0)
        weights_stage(other, later0_p, later1_p)
        return accumulate_stage(j, slot, acc), new_later0, new_later1, later0, later1

    def weights_stage(other, later0_p, later1_p):
        within = _dot(hl_ref[other], tri)
        for rows in row_chunks:
            n = rows.stop - rows.start
            later = jnp.concatenate([jnp.broadcast_to(later0_p[rows, :], (n, LANES)),
                                     jnp.broadcast_to(later1_p[rows, :], (n, LANES))], axis=1)
            w_ref[other, rows, :] = jnp.exp(lb_ref[other, rows, :] + (within[rows, :] + later)).astype(MXU_DTYPE)

    def accumulate_stage(j, slot, acc):
        v_bd = _head_block_diag(v_ref[pl.ds(block_start(j - 2), LANES), :], first_half)
        return acc + _dot(w_ref[slot], v_bd)

    def pair_step(masked, p, carry):
        carry = half_step(masked, 2 * p, 0, carry)
        return half_step(masked, 2 * p + 1, 1, carry)

    lb_ref[...] = jnp.full(lb_ref.shape, NEG, F32)
    hl_ref[...] = jnp.zeros(hl_ref.shape, MXU_DTYPE)
    w_ref[...] = jnp.zeros(w_ref.shape, MXU_DTYPE)
    zero_col = row_pos.astype(F32) * 0.0
    carry = (jnp.zeros((tq, LANES), F32), zero_col, zero_col, zero_col, zero_col)
    pairs_diag = (n_diag + 1) // 2
    pairs_full = jnp.maximum(n_blocks // 2, pairs_diag)
    pairs_valid = (n_blocks + 1) // 2

    def weights_vanished(carry):
        return (jnp.maximum(jnp.max(carry[1]), jnp.max(carry[2])) < SB_VANISH_LOG).astype(I32)

    def unmasked_pair(state):
        p, _, carry = state
        carry = pair_step(False, p, carry)
        return p + 1, weights_vanished(carry), carry

    carry = lax.fori_loop(0, pairs_diag, functools.partial(pair_step, True), carry)
    p, vanished, carry = lax.while_loop(lambda s: (s[0] < pairs_full) & (s[1] == 0), unmasked_pair,
                                        (pairs_diag, weights_vanished(carry), carry))
    p_end = jnp.where(vanished == 1, p, jnp.maximum(pairs_valid, p))
    carry = lax.fori_loop(p, p_end, functools.partial(pair_step, True), carry)
    carry = half_step(False, 2 * p_end, 0, carry, stages=(2, 3))
    carry = half_step(False, 2 * p_end + 1, 1, carry, stages=(3,))
    o_ref[...] = carry[0].astype(o_ref.dtype)


def _stick_breaking(q, k, v, tri, q_off):
    b, t, _ = q.shape
    lp = k.shape[1]
    tq = min(SB_Q_TILE, t)
    assert t % tq == 0 and lp % SB_KEY_TILE == 0
    pairs = WIDTH // LANES
    kv_spec = pl.BlockSpec((None, lp, LANES), lambda bi, pi, qi: (bi, 0, pi))
    q_spec = pl.BlockSpec((None, tq, LANES), lambda bi, pi, qi: (bi, qi, pi))
    return pl.pallas_call(
        functools.partial(_stick_breaking_kernel, q_off=q_off, tq=tq),
        grid=(b, pairs, t // tq),
        in_specs=[q_spec, kv_spec, kv_spec, pl.BlockSpec(tri.shape, lambda bi, pi, qi: (0, 0))],
        out_specs=q_spec,
        out_shape=jax.ShapeDtypeStruct(q.shape, MXU_DTYPE),
        scratch_shapes=[pltpu.VMEM((2, tq, 2 * LANES), F32),
                        pltpu.VMEM((2, tq, 4 * LANES), MXU_DTYPE),
                        pltpu.VMEM((2, tq, 2 * LANES), MXU_DTYPE)],
        compiler_params=_params(("parallel", "parallel", "arbitrary")),
        name="stick_breaking",
    )(q, k, v, tri)


def _bias_tiles_kernel(rb_ref, bucket_ref, tb_ref):
    h = pl.program_id(0)
    tiles = []
    for d in range(bucket_ref.shape[0]):
        bucket = bucket_ref[d]
        acc = jnp.zeros(bucket.shape, F32)
        for j in range(N_BUCKETS):
            acc = jnp.where(bucket == j, rb_ref[j, h], acc)
        tiles.append(acc)
    for d, tile in enumerate(tiles):
        tb_ref[d] = (tile - tiles[-1]) * LOG2E


def _t5_bucket(rel):
    half = N_BUCKETS // 2
    max_exact = half // 2
    base = jnp.where(rel > 0, half, 0)
    n = jnp.abs(rel)
    large = max_exact + (jnp.log(jnp.maximum(n, 1).astype(jnp.float32) / max_exact)
                         / math.log(T5_MAX_DISTANCE / max_exact) * (half - max_exact)).astype(jnp.int32)
    large = jnp.minimum(large, half - 1)
    return base + jnp.where(n < max_exact, n, large)


N_BIAS_TILES = 3
N_NEAR_TILES = 2


def _bias_tiles(rel_bias):
    assert T5_MAX_DISTANCE <= LANES
    j = jnp.arange(LANES, dtype=I32)[None, :, None]
    i = jnp.arange(LANES, dtype=I32)[None, None, :]
    d = jnp.arange(N_BIAS_TILES, dtype=I32)[:, None, None]
    bucket = _t5_bucket(j - i - LANES * d).astype(I32)
    return pl.pallas_call(
        _bias_tiles_kernel,
        grid=(N_HEADS,),
        in_specs=[pl.BlockSpec(memory_space=pltpu.SMEM),
                  pl.BlockSpec(bucket.shape, lambda h: (0, 0, 0))],
        out_specs=pl.BlockSpec((N_BIAS_TILES, LANES, LANES), lambda h: (h, 0, 0)),
        out_shape=jax.ShapeDtypeStruct((N_HEADS * N_BIAS_TILES, LANES, LANES), F32),
        compiler_params=_params(("arbitrary",)),
        name="bias_tiles",
    )(rel_bias, bucket)


def _sparse_kernel(q_ref, k_ref, vt_ref, qi_ref, kid_ref, wit_ref, tb_ref, ot_ref,
                   key_ref, mb_ref, cut_ref, *, q_off, tq, n_keys, top_k):
    kt_w = KEY_TILE
    p0 = q_off + pl.program_id(1) * tq
    lane = lax.broadcasted_iota(I32, (1, LANES), 1)
    key_row = lax.broadcasted_iota(I32, (kt_w, tq), 0)
    q_chunk = lax.shift_right_arithmetic(p0 + lax.broadcasted_iota(I32, (1, tq), 1), CHUNK_SHIFT)
    adm_end = (lax.shift_right_arithmetic(p0 + tq - 1, CHUNK_SHIFT) + 1) * CHUNK
    n_blocks = (jnp.minimum(adm_end, n_keys) + kt_w - 1) // kt_w
    half_masks = [lane < HEAD_DIM, lane >= HEAD_DIM]

    def admissible(start):
        pos = start + key_row
        return (lax.shift_right_arithmetic(pos, CHUNK_SHIFT) <= q_chunk) & (pos < n_keys)

    wit = wit_ref[...]
    qi_heads = []
    for h in range(N_HEADS):
        pair = qi_ref[:, (h // 2) * LANES:(h // 2 + 1) * LANES]
        qi_heads.append(jnp.where(half_masks[h % 2], pair, 0))

    def score_block(kb, _):
        start = pl.multiple_of(kb * kt_w, kt_w)
        kt = kid_ref[pl.ds(start, kt_w), :]
        acc = jnp.zeros((kt_w, tq), F32)
        for h in range(N_HEADS):
            acc = acc + jnp.maximum(_dot_nt(kt, qi_heads[h]), 0.0) * wit[h:h + 1, :]
        acc = jnp.where(acc == 0.0, 0.0, acc)
        bits = lax.bitcast_convert_type(acc, I32)
        key = jnp.where(bits < 0, bits ^ 0x7FFFFFFF, bits)
        key_ref[kb] = jnp.where(admissible(start), key, INT_MIN)
        return 0

    lax.fori_loop(0, n_blocks, score_block, 0)

    def count(pred):
        def blk(kb, cnt):
            for r in range(0, kt_w, COUNT_ROWS):
                hit = jnp.where(pred(key_ref[kb, r:r + COUNT_ROWS, :], kb * kt_w + r), 1.0, 0.0)
                parts = [hit[i * SUBLANES:(i + 1) * SUBLANES, :] for i in range(COUNT_ROWS // SUBLANES)]
                while len(parts) > 1:
                    parts = [a + b for a, b in zip(parts[::2], parts[1::2])]
                cnt = cnt + parts[0]
            return cnt
        cnt = lax.fori_loop(0, n_blocks, blk, jnp.zeros((SUBLANES, tq), F32))
        return jnp.sum(cnt, axis=0, keepdims=True)

    k_f = float(top_k)
    thr = jnp.where(count(lambda key, row0: key >= 0) >= k_f, 0, INT_MIN).astype(I32)

    def bit_step(i, thr):
        cand = thr | lax.shift_left(jnp.int32(1), 30 - i)
        return jnp.where(count(lambda key, row0: key >= cand) >= k_f, cand, thr)

    thr = lax.fori_loop(0, 31, bit_step, thr)

    n_ge = count(lambda key, row0: key >= thr)
    n_cols = key_ref.shape[0] * kt_w
    cut_ref[...] = jnp.full(cut_ref.shape, n_cols, I32)

    @pl.when(jnp.max(n_ge) > k_f)
    def _():
        need = k_f - count(lambda key, row0: key > thr)
        chunk_row = lax.broadcasted_iota(I32, (COUNT_ROWS, tq), 0)

        def cut_step(i, cut):
            cand = cut | lax.shift_left(jnp.int32(1), n_cols.bit_length() - 1 - i)
            n_eq = count(lambda key, row0: (key == thr) & ((row0 + chunk_row) < cand))
            return jnp.where(n_eq <= need, cand, cut)

        cut = lax.fori_loop(0, n_cols.bit_length(), cut_step, jnp.zeros((1, tq), I32))
        cut_ref[...] = jnp.broadcast_to(cut, cut_ref.shape)

    cut = cut_ref[0:1, :]

    def mask_block(kb, _):
        start = kb * kt_w
        key = key_ref[kb]
        sel = (key > thr) | ((key == thr) & ((start + key_row) < cut))
        mb_ref[kb] = jnp.where(sel & admissible(start), 0.0, NEG)
        return 0

    lax.fori_loop(0, n_blocks, mask_block, 0)

    n_pairs = WIDTH // LANES
    first_half = half_masks[0]
    top_rows = lax.broadcasted_iota(I32, (LANES, 1), 0) < HEAD_DIM

    def attend(near, kb, carry):
        start = pl.multiple_of(kb * kt_w, kt_w)
        mb = mb_ref[kb]
        q_groups = [(c, min(LANES, tq - c)) for c in range(0, tq, LANES)]
        tile_d = [[jnp.clip((p0 + c - (start + sub * LANES)) // LANES, 0, N_BIAS_TILES - 1) for c, _ in q_groups]
                  for sub in range(kt_w // LANES)]
        def pair_scores(pair):
            lanes = slice(pair * LANES, (pair + 1) * LANES)
            k_bd = _head_block_diag(k_ref[pl.ds(start, kt_w), lanes], first_half)
            return _dot_nt(k_bd, q_ref[:, lanes])

        scores = {pair: pair_scores(pair) for pair in range(n_pairs)}
        out = []
        for pair in range(n_pairs):
            lanes = slice(pair * LANES, (pair + 1) * LANES)
            m, l, acc = carry[pair]
            vt = vt_ref[lanes, pl.ds(start, kt_w)]
            vt_bd = jnp.concatenate([jnp.where(top_rows, vt, 0), jnp.where(top_rows, 0, vt)], axis=1)
            z = scores.pop(pair)
            m_new, alpha, p = [], [], []
            for hh in range(2):
                head = 2 * pair + hh
                s = z[hh * kt_w:(hh + 1) * kt_w, :] + mb
                if near:
                    bias = [jnp.concatenate([tb_ref[head * N_BIAS_TILES + d][:, :w] for d, (_, w) in zip(ds, q_groups)],
                                            axis=1) for ds in tile_d]
                    s = s + jnp.concatenate(bias, axis=0)
                m_h = jnp.maximum(m[hh], jnp.max(s, axis=0, keepdims=True))
                a_h = jnp.exp2(m[hh] - m_h)
                p_h = jnp.exp2(s - m_h)
                l = l[:hh] + (a_h * l[hh] + jnp.sum(p_h, axis=0, keepdims=True),) + l[hh + 1:]
                m_new.append(m_h)
                alpha.append(a_h)
                p.append(p_h.astype(MXU_DTYPE))
            acc = jnp.where(top_rows, alpha[0], alpha[1]) * acc + _dot(vt_bd, jnp.concatenate(p, axis=0))
            out.append((tuple(m_new), l, acc))
        return tuple(out)

    neg_row = jnp.full((1, tq), NEG, F32)
    zero_row = jnp.zeros((1, tq), F32)
    init = tuple(((neg_row, neg_row), (zero_row, zero_row), jnp.zeros((LANES, tq), F32)) for _ in range(n_pairs))
    n_far = jnp.maximum(n_blocks - N_NEAR_TILES, 0)
    carry = lax.fori_loop(0, n_far, functools.partial(attend, False), init)
    final = lax.fori_loop(n_far, n_blocks, functools.partial(attend, True), carry)
    for pair in range(n_pairs):
        _, l, acc = final[pair]
        ot_ref[pair * LANES:(pair + 1) * LANES, :] = (acc / jnp.where(top_rows, l[0], l[1])).astype(ot_ref.dtype)


def _sparse_attention(q, k, v, qi, kid, wi, tiles, q_off, n_keys):
    b, t, _ = q.shape
    lp = k.shape[1]
    tq = min(Q_TILE, t)
    assert t % tq == 0 and lp % KEY_TILE == 0 and q_off % LANES == 0 and (tq % LANES == 0 or t == tq)
    assert LANES + tq <= KEY_TILE
    top_k = min(TOPK_MAX, n_keys // 4)
    n_blocks = lp // KEY_TILE
    vt = v.transpose(0, 2, 1)
    wit = wi[:, :, :N_HEADS].transpose(0, 2, 1)
    q_spec = lambda w: pl.BlockSpec((None, tq, w), lambda bi, qi_: (bi, qi_, 0))
    k_spec = lambda w: pl.BlockSpec((None, lp, w), lambda bi, qi_: (bi, 0, 0))
    qt_spec = lambda rows: pl.BlockSpec((None, rows, tq), lambda bi, qi_: (bi, 0, qi_))
    out_t = pl.pallas_call(
        functools.partial(_sparse_kernel, q_off=q_off, tq=tq, n_keys=n_keys, top_k=top_k),
        grid=(b, t // tq),
        in_specs=[q_spec(WIDTH), k_spec(WIDTH), pl.BlockSpec((None, WIDTH, lp), lambda bi, qi_: (bi, 0, 0)),
                  q_spec(WIDTH), k_spec(LANES), qt_spec(N_HEADS),
                  pl.BlockSpec(tiles.shape, lambda bi, qi_: (0, 0, 0))],
        out_specs=qt_spec(WIDTH),
        out_shape=jax.ShapeDtypeStruct((b, WIDTH, t), MXU_DTYPE),
        scratch_shapes=[pltpu.VMEM((n_blocks, KEY_TILE, tq), I32),
                        pltpu.VMEM((n_blocks, KEY_TILE, tq), F32),
                        pltpu.VMEM((SUBLANES, tq), I32)],
        compiler_params=_params(("parallel", "arbitrary")),
        name="sparse_attention",
    )(q, k, vt, qi, kid, wit, tiles)
    return out_t.transpose(0, 2, 1)


def _ffn_kernel(x_ref, oa_ref, ob_ref, wo_ref, g2_ref, wup_ref, cw_ref, cb_ref, wdn_ref, st_ref,
                y_ref, cs_ref, x2_ref, h2_ref, acc_ref, prev_ref, u_ref, *, tm, n_chunks):
    keep = CONV_WIDTH - 1

    @pl.when(pl.program_id(1) == 0)
    def _():
        prev_ref[:, :, SUBLANES - keep:, :] = st_ref[...]

    mix = jnp.concatenate([oa_ref[...], ob_ref[...]], axis=1)
    x2 = x_ref[...] + _dot(mix, wo_ref[...])
    x2_ref[...] = x2
    ms = jnp.mean(x2 * x2, axis=-1, keepdims=True)
    h2_ref[...] = (x2 * lax.rsqrt(ms + EPS) * g2_ref[...]).astype(MXU_DTYPE)
    acc_ref[...] = jnp.zeros(acc_ref.shape, F32)
    row = lax.broadcasted_iota(I32, (tm, 1), 0)

    def up_proj(c, slot):
        h2 = h2_ref[...]
        for half in range(2):
            u_ref[slot, half] = _dot(h2, wup_ref[half, c])

    def finish_chunk(c, slot):
        conv = []
        for half in range(2):
            u = u_ref[slot, half]
            prev = prev_ref[half, c]
            before2, before1 = prev[SUBLANES - 2:SUBLANES - 1, :], prev[SUBLANES - 1:, :]
            u1 = jnp.where(row == 0, before1, pltpu.roll(u, 1, 0))
            u2 = jnp.where(row == 0, before2, jnp.where(row == 1, before1, pltpu.roll(u, 2, 0)))
            w = cw_ref[half, c]
            conv.append(cb_ref[half, c] + u2 * w[0:1, :] + u1 * w[1:2, :] + u * w[2:3, :])
            prev_ref[half, c] = u[tm - SUBLANES:, :]
            cs_ref[half, c] = u[tm - keep:, :]
        a, g = conv
        act = g * (1.0 / (1.0 + jnp.exp(-g))) * a
        acc_ref[...] += _dot(act.astype(MXU_DTYPE), wdn_ref[c])

    def chunk_pair(p, _):
        c = 2 * p
        up_proj(c + 1, 1)
        finish_chunk(c, 0)
        up_proj(c + 2, 0)
        finish_chunk(c + 1, 1)
        return 0

    up_proj(0, 0)
    n_pairs = (n_chunks - 1) // 2
    lax.fori_loop(0, n_pairs, chunk_pair, 0)
    last = 2 * n_pairs
    if last + 1 < n_chunks:
        up_proj(last + 1, 1)
        finish_chunk(last, 0)
        finish_chunk(last + 1, 1)
    else:
        finish_chunk(last, 0)
    y_ref[...] = x2_ref[...] + acc_ref[...]


def _ffn(x, oa, ob, w_out, g2, w_up, conv_w, conv_b, w_down, state):
    b, t, d = x.shape
    tm = min(ROW_TILE, t)
    assert t % tm == 0 and tm >= SUBLANES and CONV_WIDTH == 3
    _, n_chunks, _, fc = w_up.shape
    full = lambda a: pl.BlockSpec(a.shape, lambda bi, ti: (0,) * a.ndim)
    row_spec = lambda w: pl.BlockSpec((None, tm, w), lambda bi, ti: (bi, ti, 0))
    st_spec = pl.BlockSpec((None,) + state.shape[1:], lambda bi, ti: (bi, 0, 0, 0, 0))
    return pl.pallas_call(
        functools.partial(_ffn_kernel, tm=tm, n_chunks=n_chunks),
        grid=(b, t // tm),
        in_specs=[row_spec(d), row_spec(WIDTH), row_spec(WIDTH), full(w_out), full(g2), full(w_up),
                  full(conv_w), full(conv_b), full(w_down), st_spec],
        out_specs=[row_spec(d), st_spec],
        out_shape=[jax.ShapeDtypeStruct(x.shape, F32), jax.ShapeDtypeStruct(state.shape, F32)],
        scratch_shapes=[pltpu.VMEM((tm, d), F32), pltpu.VMEM((tm, d), MXU_DTYPE), pltpu.VMEM((tm, d), F32),
                        pltpu.VMEM((2, n_chunks, SUBLANES, fc), F32),
                        pltpu.VMEM((2, 2, tm, fc), F32)],
        compiler_params=_params(("parallel", "arbitrary")),
        name="ffn",
    )(x, oa, ob, w_out, g2, w_up, conv_w, conv_b, w_down, state)


def _cache_rows_kernel(c_ref, new_ref, o_ref, *, n_cache_tiles, t_new):
    i = pl.program_id(1)

    @pl.when(i < n_cache_tiles)
    def _():
        o_ref[...] = jnp.concatenate([c_ref[:, h, :] for h in range(N_HEADS)], axis=1).astype(o_ref.dtype)

    @pl.when(i >= n_cache_tiles)
    def _():
        o_ref[...] = jnp.zeros(o_ref.shape, o_ref.dtype)

    @pl.when(i == n_cache_tiles)
    def _():
        o_ref[:t_new, :] = new_ref[...]


def _cache_rows(cache, layer, new_rows, lp):
    _, b, past, heads, dh = cache.shape
    t_new = new_rows.shape[1]
    assert past % CACHE_TILE == 0 and lp % CACHE_TILE == 0 and t_new <= CACHE_TILE and heads == N_HEADS
    n_cache_tiles = past // CACHE_TILE
    return pl.pallas_call(
        functools.partial(_cache_rows_kernel, n_cache_tiles=n_cache_tiles, t_new=t_new),
        grid=(b, lp // CACHE_TILE),
        in_specs=[pl.BlockSpec((None, None, CACHE_TILE, heads, dh),
                               lambda bi, i: (layer, bi, jnp.minimum(i, n_cache_tiles - 1), 0, 0)),
                  pl.BlockSpec((None, t_new, heads * dh), lambda bi, i: (bi, 0, 0))],
        out_specs=pl.BlockSpec((None, CACHE_TILE, heads * dh), lambda bi, i: (bi, i, 0)),
        out_shape=jax.ShapeDtypeStruct((b, lp, heads * dh), MXU_DTYPE),
        compiler_params=_params(("parallel", "arbitrary")),
        name="cache_rows",
    )(cache, new_rows)


def _pad_keys(a, lp):
    return jnp.pad(a, ((0, 0), (0, lp - a.shape[1]), (0, 0)))


def _state_to_chunks(s, fc):
    b, keep, f2 = s.shape
    return s.reshape(b, keep, 2, f2 // 2 // fc, fc).transpose(0, 2, 3, 1, 4)


def _state_from_chunks(s):
    b, _, n_chunks, keep, fc = s.shape
    return s.transpose(0, 3, 1, 2, 4).reshape(b, keep, 2 * n_chunks * fc)


def _layer(x, caches, lw, consts):
    b, t, d = x.shape
    tri, seg, tiles = consts
    outs = _in_proj(x.reshape(b * t, d), lw["g1"], lw["w_main"], lw["w_tail"], lw["gq"], lw["gk"], seg)
    ak, av, bk, bv, ik = [o.reshape(b, t, -1) for o in outs[:5]]
    qa, ka, va, qb, kb, vb, qi, kid, wi = [o.reshape(b, t, -1) for o in outs[5:]]
    fc = lw["w_up"].shape[-1]
    if caches is None:
        n_keys = t
        lp = -(-n_keys // KEY_TILE) * KEY_TILE
        ka, va, kb, vb, kid = [_pad_keys(a, lp) for a in (ka, va, kb, vb, kid)]
        state = jnp.zeros((b, 2, lw["w_up"].shape[1], CONV_WIDTH - 1, fc), F32)
    else:
        layer, c_ak, c_av, c_bk, c_bv, c_ik, c_conv = caches
        n_keys = c_ak.shape[2] + t
        lp = -(-n_keys // KEY_TILE) * KEY_TILE
        ka, va, kb, vb = [_cache_rows(c, layer, new, lp) for c, new in ((c_ak, ka), (c_av, va), (c_bk, kb), (c_bv, vb))]
        c_ik = c_ik.astype(MXU_DTYPE)
        kid = _pad_keys(jnp.concatenate([jnp.concatenate([c_ik, c_ik], axis=-1), kid], axis=1), lp)
        state = _state_to_chunks(c_conv, fc)
    past = n_keys - t
    oa = _stick_breaking(qa, ka, va, tri, past)
    ob = _sparse_attention(qb, kb, vb, qi, kid, wi, tiles, past, n_keys)
    y, new_state = _ffn(x, oa, ob, lw["w_out"], lw["g2"], lw["w_up"], lw["conv_w"], lw["conv_b"],
                        lw["w_down"], state)
    heads = lambda a: a.reshape(b, t, N_HEADS, HEAD_DIM)
    return y, (heads(ak), heads(av), heads(bk), heads(bv), ik, _state_from_chunks(new_state))


def _layer_weights(l, norm1, w_in, q_norm, k_norm, w_out, norm2, w_up, conv_w, conv_b, w_down):
    d = w_in.shape[1]
    f = w_down.shape[1]
    fc = FF_TILE if f % FF_TILE == 0 else LANES
    assert f % fc == 0 and w_in.shape[2] == 7 * WIDTH + IDX_DIM + N_HEADS
    n_chunks = f // fc
    w = w_in[l]
    main = 7 * WIDTH
    w_ik, w_iw = w[:, main:main + IDX_DIM], w[:, main + IDX_DIM:]
    w_tail = jnp.concatenate([w_ik, w_ik, w_iw, jnp.zeros((d, LANES - N_HEADS), w.dtype)], axis=1)
    return {
        "g1": norm1[l][None, :],
        "w_main": w[:, :main].astype(MXU_DTYPE),
        "w_tail": w_tail.astype(MXU_DTYPE),
        "gq": jnp.tile(q_norm[l], N_HEADS)[None, :],
        "gk": jnp.tile(k_norm[l], N_HEADS)[None, :],
        "w_out": w_out[l].astype(MXU_DTYPE),
        "g2": norm2[l][None, :],
        "w_up": w_up[l].astype(MXU_DTYPE).reshape(d, 2, n_chunks, fc).transpose(1, 2, 0, 3),
        "conv_w": conv_w[l].reshape(CONV_WIDTH, 2, n_chunks, fc).transpose(1, 2, 0, 3),
        "conv_b": conv_b[l].reshape(2, n_chunks, 1, fc),
        "w_down": w_down[l].astype(MXU_DTYPE).reshape(n_chunks, fc, d),
    }


def kernel(x_prompt, x_sample, cache_a_k, cache_a_v, cache_b_k, cache_b_v, cache_idx_k, state_ffn_conv,
           rel_bias, norm1, w_in, q_norm, k_norm, w_out, norm2, w_up, conv_w, conv_b, w_down):
    depth = w_in.shape[0]
    assert cache_a_k.shape[3:] == (N_HEADS, HEAD_DIM) and cache_idx_k.shape[-1] == IDX_DIM
    r2 = jnp.arange(2 * LANES, dtype=I32)
    tri = ((r2[:, None] > r2[None, :]) & (r2[:, None] // LANES == r2[None, :] // LANES)).astype(MXU_DTYPE)
    tri = jnp.concatenate([tri, tri], axis=0)
    g = jnp.arange(WIDTH, dtype=I32) // HEAD_DIM
    seg = (g[:, None] == g[None, :]).astype(MXU_DTYPE)
    seg = jnp.concatenate([seg, seg], axis=0)
    consts = (tri, seg, _bias_tiles(rel_bias))

    yp, ys = x_prompt, x_sample
    p_states, s_states = [], []
    for l in range(depth):
        lw = _layer_weights(l, norm1, w_in, q_norm, k_norm, w_out, norm2, w_up, conv_w, conv_b, w_down)
        yp, st_p = _layer(yp, None, lw, consts)
        caches = (l, cache_a_k, cache_a_v, cache_b_k, cache_b_v, cache_idx_k[l], state_ffn_conv[l])
        ys, st_s = _layer(ys, caches, lw, consts)
        p_states.append(st_p)
        s_states.append(st_s)
    stack = lambda sts, i: jnp.stack([s[i] for s in sts], axis=0)
    return (yp, ys) + tuple(stack(p_states, i) for i in range(6)) + tuple(stack(s_states, i) for i in range(6))
```

```python
import functools
import math

import jax
import jax.numpy as jnp
from jax import lax
from jax.experimental import pallas as pl
from jax.experimental.pallas import tpu as pltpu

HEAD_DIM = 64
N_HEADS = 8
IDX_DIM = 64
WIDTH = N_HEADS * HEAD_DIM
CHUNK = 64
CHUNK_SHIFT = 6
TOPK_MAX = 256
N_BUCKETS = 32
T5_MAX_DISTANCE = 128
CONV_WIDTH = 3
EPS = 1e-6

LANES = 128
SUBLANES = 8
Q_TILE = 256
COUNT_ROWS = 64
SB_Q_TILE = 256
SB_KEY_TILE = 256
SB_ROW_CHUNK = 32
SB_VANISH_LOG = -105.0
KEY_TILE = 512
ROW_TILE = 256
FF_TILE = 256
VMEM_LIMIT_BYTES = 56 * 1024 * 1024

MXU_DTYPE = jnp.bfloat16
F32 = jnp.float32
I32 = jnp.int32
INT_MIN = -(2 ** 31)
NEG = -1e30
LOG2E = 1.0 / math.log(2.0)

_NT = (((1,), (1,)), ((), ()))


def _dot(a, b):
    return jnp.dot(a, b, preferred_element_type=F32)


def _dot_nt(a, b):
    return lax.dot_general(a, b, _NT, preferred_element_type=F32)


def _split_hi_lo(x):
    hi = x.astype(MXU_DTYPE)
    lo = (x - hi.astype(F32)).astype(MXU_DTYPE)
    return jnp.concatenate([hi, lo], axis=1)


def _params(semantics):
    return pltpu.CompilerParams(dimension_semantics=semantics, vmem_limit_bytes=VMEM_LIMIT_BYTES)


N_STATE_OUTPUTS = 5


def _in_proj_kernel(x_ref, g1_ref, w_ref, wt_ref, gq_ref, gk_ref, seg_ref, *refs):
    (ak_ref, av_ref, bk_ref, bv_ref, ik_ref,
     qa_ref, ka_ref, va_ref, qb_ref, kb_ref, vb_ref, qi_ref, kid_ref, wi_ref) = refs[-14:]
    x = x_ref[...]
    ms = jnp.mean(x * x, axis=-1, keepdims=True)
    h = (x * lax.rsqrt(ms + EPS) * g1_ref[...]).astype(MXU_DTYPE)

    def proj(j):
        return _dot(h, w_ref[:, j * WIDTH:(j + 1) * WIDTH])

    def head_norm(p, g_ref):
        ss = _dot(_split_hi_lo(p * p), seg_ref[...])
        return p * lax.rsqrt(ss * (1.0 / HEAD_DIM) + EPS) * g_ref[...]

    scale = HEAD_DIM ** -0.5
    b_q, b_k = proj(3), proj(4)
    qa_ref[...] = (proj(0) * scale).astype(MXU_DTYPE)
    p = proj(1)
    ak_ref[...] = p
    ka_ref[...] = p.astype(MXU_DTYPE)
    p = proj(2)
    av_ref[...] = p
    va_ref[...] = p.astype(MXU_DTYPE)
    p = proj(5)
    bv_ref[...] = p
    vb_ref[...] = p.astype(MXU_DTYPE)
    qi_ref[...] = proj(6).astype(MXU_DTYPE)
    qb_ref[...] = (head_norm(b_q, gq_ref) * (scale * LOG2E)).astype(MXU_DTYPE)
    p = head_norm(b_k, gk_ref)
    bk_ref[...] = p
    kb_ref[...] = p.astype(MXU_DTYPE)
    t = _dot(h, wt_ref[...])
    ik_ref[...] = t[:, :IDX_DIM]
    kid_ref[...] = t[:, :LANES].astype(MXU_DTYPE)
    wi_ref[...] = t[:, LANES:] * (IDX_DIM ** -0.5 * N_HEADS ** -0.5)


def _in_proj(x2d, g1, w_main, w_tail, gq, gk, seg, layer, depth, state_rows):
    rows, d = x2d.shape
    tm = min(ROW_TILE, rows)
    assert rows % tm == 0
    row_spec = lambda w: pl.BlockSpec((tm, w), lambda i: (i, 0))
    slab_spec = lambda w: pl.BlockSpec((None, tm, w), lambda i: (layer, i, 0))
    full = lambda a: pl.BlockSpec(a.shape, lambda i: (0,) * a.ndim)
    state_out = lambda w: jax.ShapeDtypeStruct((depth, rows, w), F32)
    f32_out = lambda w: jax.ShapeDtypeStruct((rows, w), F32)
    mx_out = lambda w: jax.ShapeDtypeStruct((rows, w), MXU_DTYPE)
    operands = [x2d, g1, w_main, w_tail, gq, gk, seg]
    in_specs = [row_spec(d), full(g1), full(w_main), full(w_tail), full(gq), full(gk), full(seg)]
    aliases = {}
    if state_rows is not None:
        aliases = {len(operands) + k: k for k in range(N_STATE_OUTPUTS)}
        operands += list(state_rows)
        in_specs += [pl.BlockSpec(memory_space=pl.ANY)] * N_STATE_OUTPUTS
    return pl.pallas_call(
        _in_proj_kernel,
        grid=(rows // tm,),
        in_specs=in_specs,
        out_specs=[slab_spec(WIDTH)] * 4 + [slab_spec(IDX_DIM)] + [row_spec(WIDTH)] * 7
                  + [row_spec(LANES), row_spec(LANES)],
        out_shape=[state_out(WIDTH)] * 4 + [state_out(IDX_DIM)] + [mx_out(WIDTH)] * 7
                  + [mx_out(LANES), f32_out(LANES)],
        input_output_aliases=aliases,
        compiler_params=_params(("parallel",)),
        name="in_proj",
    )(*operands)


def _head_block_diag(x, first_half):
    return jnp.concatenate([jnp.where(first_half, x, 0), jnp.where(first_half, 0, x)], axis=0)


def _stick_breaking_kernel(q_ref, k_ref, v_ref, tri_ref, o_ref, lb_ref, hl_ref, w_ref, *, q_off, tq):
    p0 = q_off + pl.program_id(2) * tq
    q = q_ref[...]
    lane = lax.broadcasted_iota(I32, (1, LANES), 1)
    lane2 = lax.broadcasted_iota(I32, (1, 2 * LANES), 1)
    first_half = lane < HEAD_DIM
    key_off = lane2 & (LANES - 1)
    row_pos = p0 + lax.broadcasted_iota(I32, (tq, 1), 0)
    n_blocks = (p0 + tq - 1 + LANES - 1) // LANES
    n_diag = n_blocks - p0 // LANES
    tri = tri_ref[...]

    def block_start(j):
        return pl.multiple_of(jnp.clip(n_blocks - 1 - j, 0, n_blocks - 1) * LANES, LANES)

    row_chunks = [slice(r, r + min(tq, SB_ROW_CHUNK)) for r in range(0, tq, min(tq, SB_ROW_CHUNK))]
    one = jnp.minimum(n_blocks, 1).astype(F32)

    def half_step(masked, j, slot, carry, stages=(1, 2, 3)):
        acc, later0, later1, later0_p, later1_p = carry
        other = 1 - slot
        if 1 not in stages:
            if 2 in stages:
                weights_stage(other, later0_p, later1_p)
            return accumulate_stage(j, slot, acc), later0, later1, later0, later1
        start = block_start(j)
        k_bd = _head_block_diag(k_ref[pl.ds(start, LANES), :], first_half)
        z = _dot_nt(q, k_bd)
        if masked:
            key_pos = jnp.where(j < n_blocks, start, q_off + tq * pl.num_programs(2)) + key_off
        sums = ([], [])
        for rows in row_chunks:
            zc = z[rows, :]
            log_beta = jnp.minimum(zc, 0.0) - jnp.log(one + jnp.exp(-jnp.abs(zc)))
            log_1mb = log_beta - zc
            if masked:
                causal = key_pos < row_pos[rows, :]
                log_1mb = jnp.where(causal, log_1mb, 0.0)
                log_beta = jnp.where(causal, log_beta, NEG)
            hi = log_1mb.astype(MXU_DTYPE)
            lb_ref[slot, rows, :] = log_beta
            hl_ref[slot, rows, :2 * LANES] = hi
            hl_ref[slot, rows, 2 * LANES:] = (log_1mb - hi.astype(F32)).astype(MXU_DTYPE)
            sums[0].append(jnp.sum(log_1mb[:, :LANES], axis=1, keepdims=True))
            sums[1].append(jnp.sum(log_1mb[:, LANES:], axis=1, keepdims=True))
        new_later0 = later0 + jnp.concatenate(sums[0], axis=0)
        new_later1 = later1 + jnp.concatenate(sums[1], axis=0)
        weights_stage(other, later0_p, later1_p)
        return accumulate_stage(j, slot, acc), new_later0, new_later1, later0, later1

    def weights_stage(other, later0_p, later1_p):
        within = _dot(hl_ref[other], tri)
        for rows in row_chunks:
            n = rows.stop - rows.start
            later = jnp.concatenate([jnp.broadcast_to(later0_p[rows, :], (n, LANES)),
                                     jnp.broadcast_to(later1_p[rows, :], (n, LANES))], axis=1)
            w_ref[other, rows, :] = jnp.exp(lb_ref[other, rows, :] + (within[rows, :] + later)).astype(MXU_DTYPE)

    def accumulate_stage(j, slot, acc):
        v_bd = _head_block_diag(v_ref[pl.ds(block_start(j - 2), LANES), :], first_half)
        return acc + _dot(w_ref[slot], v_bd)

    def pair_step(masked, p, carry):
        carry = half_step(masked, 2 * p, 0, carry)
        return half_step(masked, 2 * p + 1, 1, carry)

    lb_ref[...] = jnp.full(lb_ref.shape, NEG, F32)
    hl_ref[...] = jnp.zeros(hl_ref.shape, MXU_DTYPE)
    w_ref[...] = jnp.zeros(w_ref.shape, MXU_DTYPE)
    zero_col = row_pos.astype(F32) * 0.0
    carry = (jnp.zeros((tq, LANES), F32), zero_col, zero_col, zero_col, zero_col)
    pairs_diag = (n_diag + 1) // 2
    pairs_full = jnp.maximum(n_blocks // 2, pairs_diag)
    pairs_valid = (n_blocks + 1) // 2

    def weights_vanished(carry):
        return (jnp.maximum(jnp.max(carry[1]), jnp.max(carry[2])) < SB_VANISH_LOG).astype(I32)

    def unmasked_pair(state):
        p, _, carry = state
        carry = pair_step(False, p, carry)
        return p + 1, weights_vanished(carry), carry

    carry = lax.fori_loop(0, pairs_diag, functools.partial(pair_step, True), carry)
    p, vanished, carry = lax.while_loop(lambda s: (s[0] < pairs_full) & (s[1] == 0), unmasked_pair,
                                        (pairs_diag, weights_vanished(carry), carry))
    p_end = jnp.where(vanished == 1, p, jnp.maximum(pairs_valid, p))
    carry = lax.fori_loop(p, p_end, functools.partial(pair_step, True), carry)
    carry = half_step(False, 2 * p_end, 0, carry, stages=(2, 3))
    carry = half_step(False, 2 * p_end + 1, 1, carry, stages=(3,))
    o_ref[...] = carry[0].astype(o_ref.dtype)


def _stick_breaking(q, k, v, tri, q_off):
    b, t, _ = q.shape
    lp = k.shape[1]
    tq = min(SB_Q_TILE, t)
    assert t % tq == 0 and lp % SB_KEY_TILE == 0
    pairs = WIDTH // LANES
    kv_spec = pl.BlockSpec((None, lp, LANES), lambda bi, pi, qi: (bi, 0, pi))
    q_spec = pl.BlockSpec((None, tq, LANES), lambda bi, pi, qi: (bi, qi, pi))
    return pl.pallas_call(
        functools.partial(_stick_breaking_kernel, q_off=q_off, tq=tq),
        grid=(b, pairs, t // tq),
        in_specs=[q_spec, kv_spec, kv_spec, pl.BlockSpec(tri.shape, lambda bi, pi, qi: (0, 0))],
        out_specs=q_spec,
        out_shape=jax.ShapeDtypeStruct(q.shape, MXU_DTYPE),
        scratch_shapes=[pltpu.VMEM((2, tq, 2 * LANES), F32),
                        pltpu.VMEM((2, tq, 4 * LANES), MXU_DTYPE),
                        pltpu.VMEM((2, tq, 2 * LANES), MXU_DTYPE)],
        compiler_params=_params(("parallel", "parallel", "arbitrary")),
        name="stick_breaking",
    )(q, k, v, tri)


def _bias_tiles_kernel(rb_ref, bucket_ref, tb_ref):
    h = pl.program_id(0)
    tiles = []
    for d in range(bucket_ref.shape[0]):
        bucket = bucket_ref[d]
        acc = jnp.zeros(bucket.shape, F32)
        for j in range(N_BUCKETS):
            acc = jnp.where(bucket == j, rb_ref[j, h], acc)
        tiles.append(acc)
    for d, tile in enumerate(tiles):
        tb_ref[d] = (tile - tiles[-1]) * LOG2E


def _t5_bucket(rel):
    half = N_BUCKETS // 2
    max_exact = half // 2
    base = jnp.where(rel > 0, half, 0)
    n = jnp.abs(rel)
    large = max_exact + (jnp.log(jnp.maximum(n, 1).astype(jnp.float32) / max_exact)
                         / math.log(T5_MAX_DISTANCE / max_exact) * (half - max_exact)).astype(jnp.int32)
    large = jnp.minimum(large, half - 1)
    return base + jnp.where(n < max_exact, n, large)


N_BIAS_TILES = 3
N_NEAR_TILES = 2


def _bias_tiles(rel_bias):
    assert T5_MAX_DISTANCE <= LANES
    j = jnp.arange(LANES, dtype=I32)[None, :, None]
    i = jnp.arange(LANES, dtype=I32)[None, None, :]
    d = jnp.arange(N_BIAS_TILES, dtype=I32)[:, None, None]
    bucket = _t5_bucket(j - i - LANES * d).astype(I32)
    return pl.pallas_call(
        _bias_tiles_kernel,
        grid=(N_HEADS,),
        in_specs=[pl.BlockSpec(memory_space=pltpu.SMEM),
                  pl.BlockSpec(bucket.shape, lambda h: (0, 0, 0))],
        out_specs=pl.BlockSpec((N_BIAS_TILES, LANES, LANES), lambda h: (h, 0, 0)),
        out_shape=jax.ShapeDtypeStruct((N_HEADS * N_BIAS_TILES, LANES, LANES), F32),
        compiler_params=_params(("arbitrary",)),
        name="bias_tiles",
    )(rel_bias, bucket)


def _sparse_kernel(q_ref, k_ref, vt_ref, qi_ref, kid_ref, wit_ref, tb_ref, ot_ref,
                   key_ref, mb_ref, cut_ref, *, q_off, tq, n_keys, top_k):
    kt_w = KEY_TILE
    p0 = q_off + pl.program_id(1) * tq
    lane = lax.broadcasted_iota(I32, (1, LANES), 1)
    key_row = lax.broadcasted_iota(I32, (kt_w, tq), 0)
    q_chunk = lax.shift_right_arithmetic(p0 + lax.broadcasted_iota(I32, (1, tq), 1), CHUNK_SHIFT)
    adm_end = (lax.shift_right_arithmetic(p0 + tq - 1, CHUNK_SHIFT) + 1) * CHUNK
    n_blocks = (jnp.minimum(adm_end, n_keys) + kt_w - 1) // kt_w
    half_masks = [lane < HEAD_DIM, lane >= HEAD_DIM]

    def admissible(start):
        pos = start + key_row
        return (lax.shift_right_arithmetic(pos, CHUNK_SHIFT) <= q_chunk) & (pos < n_keys)

    wit = wit_ref[...]
    qi_heads = []
    for h in range(N_HEADS):
        pair = qi_ref[:, (h // 2) * LANES:(h // 2 + 1) * LANES]
        qi_heads.append(jnp.where(half_masks[h % 2], pair, 0))

    def score_block(kb, _):
        start = pl.multiple_of(kb * kt_w, kt_w)
        kt = kid_ref[pl.ds(start, kt_w), :]
        acc = jnp.zeros((kt_w, tq), F32)
        for h in range(N_HEADS):
            acc = acc + jnp.maximum(_dot_nt(kt, qi_heads[h]), 0.0) * wit[h:h + 1, :]
        acc = jnp.where(acc == 0.0, 0.0, acc)
        bits = lax.bitcast_convert_type(acc, I32)
        key = jnp.where(bits < 0, bits ^ 0x7FFFFFFF, bits)
        key_ref[kb] = jnp.where(admissible(start), key, INT_MIN)
        return 0

    lax.fori_loop(0, n_blocks, score_block, 0)

    def count(pred):
        def blk(kb, cnt):
            for r in range(0, kt_w, COUNT_ROWS):
                hit = jnp.where(pred(key_ref[kb, r:r + COUNT_ROWS, :], kb * kt_w + r), 1.0, 0.0)
                parts = [hit[i * SUBLANES:(i + 1) * SUBLANES, :] for i in range(COUNT_ROWS // SUBLANES)]
                while len(parts) > 1:
                    parts = [a + b for a, b in zip(parts[::2], parts[1::2])]
                cnt = cnt + parts[0]
            return cnt
        cnt = lax.fori_loop(0, n_blocks, blk, jnp.zeros((SUBLANES, tq), F32))
        return jnp.sum(cnt, axis=0, keepdims=True)

    k_f = float(top_k)
    thr = jnp.where(count(lambda key, row0: key >= 0) >= k_f, 0, INT_MIN).astype(I32)

    def bit_step(i, thr):
        cand = thr | lax.shift_left(jnp.int32(1), 30 - i)
        return jnp.where(count(lambda key, row0: key >= cand) >= k_f, cand, thr)

    thr = lax.fori_loop(0, 31, bit_step, thr)

    n_ge = count(lambda key, row0: key >= thr)
    n_cols = key_ref.shape[0] * kt_w
    cut_ref[...] = jnp.full(cut_ref.shape, n_cols, I32)

    @pl.when(jnp.max(n_ge) > k_f)
    def _():
        need = k_f - count(lambda key, row0: key > thr)
        chunk_row = lax.broadcasted_iota(I32, (COUNT_ROWS, tq), 0)

        def cut_step(i, cut):
            cand = cut | lax.shift_left(jnp.int32(1), n_cols.bit_length() - 1 - i)
            n_eq = count(lambda key, row0: (key == thr) & ((row0 + chunk_row) < cand))
            return jnp.where(n_eq <= need, cand, cut)

        cut = lax.fori_loop(0, n_cols.bit_length(), cut_step, jnp.zeros((1, tq), I32))
        cut_ref[...] = jnp.broadcast_to(cut, cut_ref.shape)

    cut = cut_ref[0:1, :]

    def mask_block(kb, _):
        start = kb * kt_w
        key = key_ref[kb]
        sel = (key > thr) | ((key == thr) & ((start + key_row) < cut))
        mb_ref[kb] = jnp.where(sel & admissible(start), 0.0, NEG)
        return 0

    lax.fori_loop(0, n_blocks, mask_block, 0)

    n_pairs = WIDTH // LANES
    first_half = half_masks[0]
    top_rows = lax.broadcasted_iota(I32, (LANES, 1), 0) < HEAD_DIM

    def attend(near, kb, carry):
        start = pl.multiple_of(kb * kt_w, kt_w)
        mb = mb_ref[kb]
        q_groups = [(c, min(LANES, tq - c)) for c in range(0, tq, LANES)]
        tile_d = [[jnp.clip((p0 + c - (start + sub * LANES)) // LANES, 0, N_BIAS_TILES - 1) for c, _ in q_groups]
                  for sub in range(kt_w // LANES)]
        def pair_scores(pair):
            lanes = slice(pair * LANES, (pair + 1) * LANES)
            k_bd = _head_block_diag(k_ref[pl.ds(start, kt_w), lanes], first_half)
            return _dot_nt(k_bd, q_ref[:, lanes])

        scores = {pair: pair_scores(pair) for pair in range(n_pairs)}
        out = []
        for pair in range(n_pairs):
            lanes = slice(pair * LANES, (pair + 1) * LANES)
            m, l, acc = carry[pair]
            vt = vt_ref[lanes, pl.ds(start, kt_w)]
            vt_bd = jnp.concatenate([jnp.where(top_rows, vt, 0), jnp.where(top_rows, 0, vt)], axis=1)
            z = scores.pop(pair)
            m_new, alpha, p = [], [], []
            for hh in range(2):
                head = 2 * pair + hh
                s = z[hh * kt_w:(hh + 1) * kt_w, :] + mb
                if near:
                    bias = [jnp.concatenate([tb_ref[head * N_BIAS_TILES + d][:, :w] for d, (_, w) in zip(ds, q_groups)],
                                            axis=1) for ds in tile_d]
                    s = s + jnp.concatenate(bias, axis=0)
                m_h = jnp.maximum(m[hh], jnp.max(s, axis=0, keepdims=True))
                a_h = jnp.exp2(m[hh] - m_h)
                p_h = jnp.exp2(s - m_h)
                l = l[:hh] + (a_h * l[hh] + jnp.sum(p_h, axis=0, keepdims=True),) + l[hh + 1:]
                m_new.append(m_h)
                alpha.append(a_h)
                p.append(p_h.astype(MXU_DTYPE))
            acc = jnp.where(top_rows, alpha[0], alpha[1]) * acc + _dot(vt_bd, jnp.concatenate(p, axis=0))
            out.append((tuple(m_new), l, acc))
        return tuple(out)

    neg_row = jnp.full((1, tq), NEG, F32)
    zero_row = jnp.zeros((1, tq), F32)
    init = tuple(((neg_row, neg_row), (zero_row, zero_row), jnp.zeros((LANES, tq), F32)) for _ in range(n_pairs))
    n_far = jnp.maximum(n_blocks - N_NEAR_TILES, 0)
    carry = lax.fori_loop(0, n_far, functools.partial(attend, False), init)
    final = lax.fori_loop(n_far, n_blocks, functools.partial(attend, True), carry)
    for pair in range(n_pairs):
        _, l, acc = final[pair]
        ot_ref[pair * LANES:(pair + 1) * LANES, :] = (acc / jnp.where(top_rows, l[0], l[1])).astype(ot_ref.dtype)


def _sparse_attention(q, k, v, qi, kid, wi, tiles, q_off, n_keys):
    b, t, _ = q.shape
    lp = k.shape[1]
    tq = min(Q_TILE, t)
    assert t % tq == 0 and lp % KEY_TILE == 0 and q_off % LANES == 0 and (tq % LANES == 0 or t == tq)
    assert LANES + tq <= KEY_TILE
    top_k = min(TOPK_MAX, n_keys // 4)
    n_blocks = lp // KEY_TILE
    vt = v.transpose(0, 2, 1)
    wit = wi[:, :, :N_HEADS].transpose(0, 2, 1)
    q_spec = lambda w: pl.BlockSpec((None, tq, w), lambda bi, qi_: (bi, qi_, 0))
    k_spec = lambda w: pl.BlockSpec((None, lp, w), lambda bi, qi_: (bi, 0, 0))
    qt_spec = lambda rows: pl.BlockSpec((None, rows, tq), lambda bi, qi_: (bi, 0, qi_))
    out_t = pl.pallas_call(
        functools.partial(_sparse_kernel, q_off=q_off, tq=tq, n_keys=n_keys, top_k=top_k),
        grid=(b, t // tq),
        in_specs=[q_spec(WIDTH), k_spec(WIDTH), pl.BlockSpec((None, WIDTH, lp), lambda bi, qi_: (bi, 0, 0)),
                  q_spec(WIDTH), k_spec(LANES), qt_spec(N_HEADS),
                  pl.BlockSpec(tiles.shape, lambda bi, qi_: (0, 0, 0))],
        out_specs=qt_spec(WIDTH),
        out_shape=jax.ShapeDtypeStruct((b, WIDTH, t), MXU_DTYPE),
        scratch_shapes=[pltpu.VMEM((n_blocks, KEY_TILE, tq), I32),
                        pltpu.VMEM((n_blocks, KEY_TILE, tq), F32),
                        pltpu.VMEM((SUBLANES, tq), I32)],
        compiler_params=_params(("parallel", "arbitrary")),
        name="sparse_attention",
    )(q, k, vt, qi, kid, wit, tiles)
    return out_t.transpose(0, 2, 1)


def _ffn_kernel(x_ref, oa_ref, ob_ref, wo_ref, g2_ref, wup_ref, cw_ref, cb_ref, wdn_ref, st_ref,
                y_ref, cs_ref, x2_ref, h2_ref, acc_ref, prev_ref, u_ref, *, tm, n_chunks):
    keep = CONV_WIDTH - 1

    @pl.when(pl.program_id(1) == 0)
    def _():
        prev_ref[:, :, SUBLANES - keep:, :] = st_ref[...]

    mix = jnp.concatenate([oa_ref[...], ob_ref[...]], axis=1)
    x2 = x_ref[...] + _dot(mix, wo_ref[...])
    x2_ref[...] = x2
    ms = jnp.mean(x2 * x2, axis=-1, keepdims=True)
    h2_ref[...] = (x2 * lax.rsqrt(ms + EPS) * g2_ref[...]).astype(MXU_DTYPE)
    acc_ref[...] = jnp.zeros(acc_ref.shape, F32)
    row = lax.broadcasted_iota(I32, (tm, 1), 0)

    def up_proj(c, slot):
        h2 = h2_ref[...]
        for half in range(2):
            u_ref[slot, half] = _dot(h2, wup_ref[half, c])

    def finish_chunk(c, slot):
        conv = []
        for half in range(2):
            u = u_ref[slot, half]
            prev = prev_ref[half, c]
            before2, before1 = prev[SUBLANES - 2:SUBLANES - 1, :], prev[SUBLANES - 1:, :]
            u1 = jnp.where(row == 0, before1, pltpu.roll(u, 1, 0))
            u2 = jnp.where(row == 0, before2, jnp.where(row == 1, before1, pltpu.roll(u, 2, 0)))
            w = cw_ref[half, c]
            conv.append(cb_ref[half, c] + u2 * w[0:1, :] + u1 * w[1:2, :] + u * w[2:3, :])
            prev_ref[half, c] = u[tm - SUBLANES:, :]
            cs_ref[half, c] = u[tm - keep:, :]
        a, g = conv
        act = g * (1.0 / (1.0 + jnp.exp(-g))) * a
        acc_ref[...] += _dot(act.astype(MXU_DTYPE), wdn_ref[c])

    def chunk_pair(p, _):
        c = 2 * p
        up_proj(c + 1, 1)
        finish_chunk(c, 0)
        up_proj(c + 2, 0)
        finish_chunk(c + 1, 1)
        return 0

    up_proj(0, 0)
    n_pairs = (n_chunks - 1) // 2
    lax.fori_loop(0, n_pairs, chunk_pair, 0)
    last = 2 * n_pairs
    if last + 1 < n_chunks:
        up_proj(last + 1, 1)
        finish_chunk(last, 0)
        finish_chunk(last + 1, 1)
    else:
        finish_chunk(last, 0)
    y_ref[...] = x2_ref[...] + acc_ref[...]


def _ffn(x, oa, ob, w_out, g2, w_up, conv_w, conv_b, w_down, state):
    b, t, d = x.shape
    tm = min(ROW_TILE, t)
    assert t % tm == 0 and tm >= SUBLANES and CONV_WIDTH == 3
    _, n_chunks, _, fc = w_up.shape
    full = lambda a: pl.BlockSpec(a.shape, lambda bi, ti: (0,) * a.ndim)
    row_spec = lambda w: pl.BlockSpec((None, tm, w), lambda bi, ti: (bi, ti, 0))
    st_spec = pl.BlockSpec((None,) + state.shape[1:], lambda bi, ti: (bi, 0, 0, 0, 0))
    return pl.pallas_call(
        functools.partial(_ffn_kernel, tm=tm, n_chunks=n_chunks),
        grid=(b, t // tm),
        in_specs=[row_spec(d), row_spec(WIDTH), row_spec(WIDTH), full(w_out), full(g2), full(w_up),
                  full(conv_w), full(conv_b), full(w_down), st_spec],
        out_specs=[row_spec(d), st_spec],
        out_shape=[jax.ShapeDtypeStruct(x.shape, F32), jax.ShapeDtypeStruct(state.shape, F32)],
        scratch_shapes=[pltpu.VMEM((tm, d), F32), pltpu.VMEM((tm, d), MXU_DTYPE), pltpu.VMEM((tm, d), F32),
                        pltpu.VMEM((2, n_chunks, SUBLANES, fc), F32),
                        pltpu.VMEM((2, 2, tm, fc), F32)],
        compiler_params=_params(("parallel", "arbitrary")),
        name="ffn",
    )(x, oa, ob, w_out, g2, w_up, conv_w, conv_b, w_down, state)


def _pad_keys(a, lp):
    return jnp.pad(a, ((0, 0), (0, lp - a.shape[1]), (0, 0)))


def _state_to_chunks(s, fc):
    b, keep, f2 = s.shape
    return s.reshape(b, keep, 2, f2 // 2 // fc, fc).transpose(0, 2, 3, 1, 4)


def _state_from_chunks(s):
    b, _, n_chunks, keep, fc = s.shape
    return s.transpose(0, 3, 1, 2, 4).reshape(b, keep, 2 * n_chunks * fc)


def _layer(x, caches, lw, consts, layer, depth, state_rows):
    b, t, d = x.shape
    tri, seg, tiles = consts
    outs = _in_proj(x.reshape(b * t, d), lw["g1"], lw["w_main"], lw["w_tail"], lw["gq"], lw["gk"], seg,
                    layer, depth, state_rows)
    state_rows = outs[:N_STATE_OUTPUTS]
    qa, ka, va, qb, kb, vb, qi, kid, wi = [o.reshape(b, t, -1) for o in outs[N_STATE_OUTPUTS:]]
    fc = lw["w_up"].shape[-1]
    if caches is None:
        past = 0
        state = jnp.zeros((b, 2, lw["w_up"].shape[1], CONV_WIDTH - 1, fc), F32)
    else:
        c_ak, c_av, c_bk, c_bv, c_ik, c_conv = caches
        past = c_ak.shape[1]
        flat = lambda c: c.reshape(b, past, -1).astype(MXU_DTYPE)
        ka = jnp.concatenate([flat(c_ak), ka], axis=1)
        va = jnp.concatenate([flat(c_av), va], axis=1)
        kb = jnp.concatenate([flat(c_bk), kb], axis=1)
        vb = jnp.concatenate([flat(c_bv), vb], axis=1)
        kid = jnp.concatenate([jnp.concatenate([flat(c_ik)] * 2, axis=-1), kid], axis=1)
        state = _state_to_chunks(c_conv, fc)
    n_keys = past + t
    lp = -(-n_keys // KEY_TILE) * KEY_TILE
    ka, va, kb, vb, kid = [_pad_keys(a, lp) for a in (ka, va, kb, vb, kid)]
    oa = _stick_breaking(qa, ka, va, tri, past)
    ob = _sparse_attention(qb, kb, vb, qi, kid, wi, tiles, past, n_keys)
    y, new_state = _ffn(x, oa, ob, lw["w_out"], lw["g2"], lw["w_up"], lw["conv_w"], lw["conv_b"],
                        lw["w_down"], state)
    return y, state_rows, _state_from_chunks(new_state)


def _layer_weights(l, norm1, w_in, q_norm, k_norm, w_out, norm2, w_up, conv_w, conv_b, w_down):
    d = w_in.shape[1]
    f = w_down.shape[1]
    fc = FF_TILE if f % FF_TILE == 0 else LANES
    assert f % fc == 0 and w_in.shape[2] == 7 * WIDTH + IDX_DIM + N_HEADS
    n_chunks = f // fc
    w = w_in[l]
    main = 7 * WIDTH
    w_ik, w_iw = w[:, main:main + IDX_DIM], w[:, main + IDX_DIM:]
    w_tail = jnp.concatenate([w_ik, w_ik, w_iw, jnp.zeros((d, LANES - N_HEADS), w.dtype)], axis=1)
    return {
        "g1": norm1[l][None, :],
        "w_main": w[:, :main].astype(MXU_DTYPE),
        "w_tail": w_tail.astype(MXU_DTYPE),
        "gq": jnp.tile(q_norm[l], N_HEADS)[None, :],
        "gk": jnp.tile(k_norm[l], N_HEADS)[None, :],
        "w_out": w_out[l].astype(MXU_DTYPE),
        "g2": norm2[l][None, :],
        "w_up": w_up[l].astype(MXU_DTYPE).reshape(d, 2, n_chunks, fc).transpose(1, 2, 0, 3),
        "conv_w": conv_w[l].reshape(CONV_WIDTH, 2, n_chunks, fc).transpose(1, 2, 0, 3),
        "conv_b": conv_b[l].reshape(2, n_chunks, 1, fc),
        "w_down": w_down[l].astype(MXU_DTYPE).reshape(n_chunks, fc, d),
    }


def kernel(x_prompt, x_sample, cache_a_k, cache_a_v, cache_b_k, cache_b_v, cache_idx_k, state_ffn_conv,
           rel_bias, norm1, w_in, q_norm, k_norm, w_out, norm2, w_up, conv_w, conv_b, w_down):
    depth = w_in.shape[0]
    assert cache_a_k.shape[3:] == (N_HEADS, HEAD_DIM) and cache_idx_k.shape[-1] == IDX_DIM
    r2 = jnp.arange(2 * LANES, dtype=I32)
    tri = ((r2[:, None] > r2[None, :]) & (r2[:, None] // LANES == r2[None, :] // LANES)).astype(MXU_DTYPE)
    tri = jnp.concatenate([tri, tri], axis=0)
    g = jnp.arange(WIDTH, dtype=I32) // HEAD_DIM
    seg = (g[:, None] == g[None, :]).astype(MXU_DTYPE)
    seg = jnp.concatenate([seg, seg], axis=0)
    consts = (tri, seg, _bias_tiles(rel_bias))

    yp, ys = x_prompt, x_sample
    p_rows = s_rows = None
    p_conv, s_conv = [], []
    for l in range(depth):
        lw = _layer_weights(l, norm1, w_in, q_norm, k_norm, w_out, norm2, w_up, conv_w, conv_b, w_down)
        yp, p_rows, conv = _layer(yp, None, lw, consts, l, depth, p_rows)
        p_conv.append(conv)
        caches = (cache_a_k[l], cache_a_v[l], cache_b_k[l], cache_b_v[l], cache_idx_k[l], state_ffn_conv[l])
        ys, s_rows, conv = _layer(ys, caches, lw, consts, l, depth, s_rows)
        s_conv.append(conv)

    def states(x, rows, conv):
        b, t, _ = x.shape
        heads = [a.reshape(depth, b, t, N_HEADS, HEAD_DIM) for a in rows[:4]]
        return tuple(heads) + (rows[4].reshape(depth, b, t, IDX_DIM), jnp.stack(conv, axis=0))

    return (yp, ys) + states(x_prompt, p_rows, p_conv) + states(x_sample, s_rows, s_conv)
```

```python
import functools
import math

import jax
import jax.numpy as jnp
from jax import lax
from jax.experimental import pallas as pl
from jax.experimental.pallas import tpu as pltpu

HEAD_DIM = 64
N_HEADS = 8
IDX_DIM = 64
WIDTH = N_HEADS * HEAD_DIM
CHUNK = 64
CHUNK_SHIFT = 6
TOPK_MAX = 256
N_BUCKETS = 32
T5_MAX_DISTANCE = 128
CONV_WIDTH = 3
EPS = 1e-6

LANES = 128
SUBLANES = 8
Q_TILE = 256
COUNT_ROWS = 64
SB_Q_TILE = 256
SB_KEY_TILE = 256
SB_ROW_CHUNK = 32
SB_VANISH_LOG = -105.0
KEY_TILE = 512
ROW_TILE = 256
FF_TILE = 256
VMEM_LIMIT_BYTES = 56 * 1024 * 1024

MXU_DTYPE = jnp.bfloat16
F32 = jnp.float32
I32 = jnp.int32
INT_MIN = -(2 ** 31)
NEG = -1e30
LOG2E = 1.0 / math.log(2.0)

_NT = (((1,), (1,)), ((), ()))


def _dot(a, b):
    return jnp.dot(a, b, preferred_element_type=F32)


def _dot_nt(a, b):
    return lax.dot_general(a, b, _NT, preferred_element_type=F32)


def _split_hi_lo(x):
    hi = x.astype(MXU_DTYPE)
    lo = (x - hi.astype(F32)).astype(MXU_DTYPE)
    return jnp.concatenate([hi, lo], axis=1)


def _params(semantics):
    return pltpu.CompilerParams(dimension_semantics=semantics, vmem_limit_bytes=VMEM_LIMIT_BYTES)


N_STATE_OUTPUTS = 5


def _in_proj_kernel(x_ref, g1_ref, w_ref, wt_ref, gq_ref, gk_ref, seg_ref, *refs):
    (ak_ref, av_ref, bk_ref, bv_ref, ik_ref,
     qa_ref, ka_ref, va_ref, qb_ref, kb_ref, vb_ref, qi_ref, kid_ref, wi_ref) = refs[-14:]
    x = x_ref[...]
    ms = jnp.mean(x * x, axis=-1, keepdims=True)
    h = (x * lax.rsqrt(ms + EPS) * g1_ref[...]).astype(MXU_DTYPE)

    def proj(j):
        return _dot(h, w_ref[:, j * WIDTH:(j + 1) * WIDTH])

    def head_norm(p, g_ref):
        ss = _dot(_split_hi_lo(p * p), seg_ref[...])
        return p * lax.rsqrt(ss * (1.0 / HEAD_DIM) + EPS) * g_ref[...]

    scale = HEAD_DIM ** -0.5
    b_q, b_k = proj(3), proj(4)
    qa_ref[...] = (proj(0) * scale).astype(MXU_DTYPE)
    p = proj(1)
    ak_ref[...] = p
    ka_ref[...] = p.astype(MXU_DTYPE)
    p = proj(2)
    av_ref[...] = p
    va_ref[...] = p.astype(MXU_DTYPE)
    p = proj(5)
    bv_ref[...] = p
    vb_ref[...] = p.astype(MXU_DTYPE)
    qi_ref[...] = proj(6).astype(MXU_DTYPE)
    qb_ref[...] = (head_norm(b_q, gq_ref) * (scale * LOG2E)).astype(MXU_DTYPE)
    p = head_norm(b_k, gk_ref)
    bk_ref[...] = p
    kb_ref[...] = p.astype(MXU_DTYPE)
    t = _dot(h, wt_ref[...])
    ik_ref[...] = t[:, :IDX_DIM]
    kid_ref[...] = t[:, :LANES].astype(MXU_DTYPE)
    wi_ref[...] = t[:, LANES:] * (IDX_DIM ** -0.5 * N_HEADS ** -0.5)


def _in_proj(x2d, g1, w_main, w_tail, gq, gk, seg, layer, depth, state_rows, t):
    rows, d = x2d.shape
    tm = min(ROW_TILE, rows)
    assert rows % tm == 0
    row_spec = lambda w: pl.BlockSpec((tm, w), lambda i: (i, 0))
    full = lambda a: pl.BlockSpec(a.shape, lambda i: (0,) * a.ndim)
    if t % tm == 0:
        per_seq = t // tm
        slab_spec = lambda w: pl.BlockSpec((None, None, tm, w), lambda i: (layer, i // per_seq, i % per_seq, 0))
        state_out = lambda w: jax.ShapeDtypeStruct((depth, rows // t, t, w), F32)
    else:
        slab_spec = lambda w: pl.BlockSpec((None, tm, w), lambda i: (layer, i, 0))
        state_out = lambda w: jax.ShapeDtypeStruct((depth, rows, w), F32)
    f32_out = lambda w: jax.ShapeDtypeStruct((rows, w), F32)
    mx_out = lambda w: jax.ShapeDtypeStruct((rows, w), MXU_DTYPE)
    operands = [x2d, g1, w_main, w_tail, gq, gk, seg]
    in_specs = [row_spec(d), full(g1), full(w_main), full(w_tail), full(gq), full(gk), full(seg)]
    aliases = {}
    if state_rows is not None:
        aliases = {len(operands) + k: k for k in range(N_STATE_OUTPUTS)}
        operands += list(state_rows)
        in_specs += [pl.BlockSpec(memory_space=pl.ANY)] * N_STATE_OUTPUTS
    return pl.pallas_call(
        _in_proj_kernel,
        grid=(rows // tm,),
        in_specs=in_specs,
        out_specs=[slab_spec(WIDTH)] * 4 + [slab_spec(IDX_DIM)] + [row_spec(WIDTH)] * 7
                  + [row_spec(LANES), row_spec(LANES)],
        out_shape=[state_out(WIDTH)] * 4 + [state_out(IDX_DIM)] + [mx_out(WIDTH)] * 7
                  + [mx_out(LANES), f32_out(LANES)],
        input_output_aliases=aliases,
        compiler_params=_params(("parallel",)),
        name="in_proj",
    )(*operands)


def _head_block_diag(x, first_half):
    return jnp.concatenate([jnp.where(first_half, x, 0), jnp.where(first_half, 0, x)], axis=0)


def _stick_breaking_kernel(q_ref, k_ref, v_ref, tri_ref, o_ref, lb_ref, hl_ref, w_ref, *, q_off, tq):
    p0 = q_off + pl.program_id(2) * tq
    q = q_ref[...]
    lane = lax.broadcasted_iota(I32, (1, LANES), 1)
    lane2 = lax.broadcasted_iota(I32, (1, 2 * LANES), 1)
    first_half = lane < HEAD_DIM
    key_off = lane2 & (LANES - 1)
    row_pos = p0 + lax.broadcasted_iota(I32, (tq, 1), 0)
    n_blocks = (p0 + tq - 1 + LANES - 1) // LANES
    n_diag = n_blocks - p0 // LANES
    tri = tri_ref[...]

    def block_start(j):
        return pl.multiple_of(jnp.clip(n_blocks - 1 - j, 0, n_blocks - 1) * LANES, LANES)

    row_chunks = [slice(r, r + min(tq, SB_ROW_CHUNK)) for r in range(0, tq, min(tq, SB_ROW_CHUNK))]
    one = jnp.minimum(n_blocks, 1).astype(F32)

    def half_step(masked, j, slot, carry, stages=(1, 2, 3)):
        acc, later0, later1, later0_p, later1_p = carry
        other = 1 - slot
        if 1 not in stages:
            if 2 in stages:
                weights_stage(other, later0_p, later1_p)
            return accumulate_stage(j, slot, acc), later0, later1, later0, later1
        start = block_start(j)
        k_bd = _head_block_diag(k_ref[pl.ds(start, LANES), :], first_half)
        z = _dot_nt(q, k_bd)
        if masked:
            key_pos = jnp.where(j < n_blocks, start, q_off + tq * pl.num_programs(2)) + key_off
        sums = ([], [])
        for rows in row_chunks:
            zc = z[rows, :]
            log_beta = jnp.minimum(zc, 0.0) - jnp.log(one + jnp.exp(-jnp.abs(zc)))
            log_1mb = log_beta - zc
            if masked:
                causal = key_pos < row_pos[rows, :]
                log_1mb = jnp.where(causal, log_1mb, 0.0)
                log_beta = jnp.where(causal, log_beta, NEG)
            hi = log_1mb.astype(MXU_DTYPE)
            lb_ref[slot, rows, :] = log_beta
            hl_ref[slot, rows, :2 * LANES] = hi
            hl_ref[slot, rows, 2 * LANES:] = (log_1mb - hi.astype(F32)).astype(MXU_DTYPE)
            sums[0].append(jnp.sum(log_1mb[:, :LANES], axis=1, keepdims=True))
            sums[1].append(jnp.sum(log_1mb[:, LANES:], axis=1, keepdims=True))
        new_later0 = later0 + jnp.concatenate(sums[0], axis=0)
        new_later1 = later1 + jnp.concatenate(sums[1], axis=0)
        weights_stage(other, later0_p, later1_p)
        return accumulate_stage(j, slot, acc), new_later0, new_later1, later0, later1

    def weights_stage(other, later0_p, later1_p):
        within = _dot(hl_ref[other], tri)
        for rows in row_chunks:
            n = rows.stop - rows.start
            later = jnp.concatenate([jnp.broadcast_to(later0_p[rows, :], (n, LANES)),
                                     jnp.broadcast_to(later1_p[rows, :], (n, LANES))], axis=1)
            w_ref[other, rows, :] = jnp.exp(lb_ref[other, rows, :] + (within[rows, :] + later)).astype(MXU_DTYPE)

    def accumulate_stage(j, slot, acc):
        v_bd = _head_block_diag(v_ref[pl.ds(block_start(j - 2), LANES), :], first_half)
        return acc + _dot(w_ref[slot], v_bd)

    def pair_step(masked, p, carry):
        carry = half_step(masked, 2 * p, 0, carry)
        return half_step(masked, 2 * p + 1, 1, carry)

    lb_ref[...] = jnp.full(lb_ref.shape, NEG, F32)
    hl_ref[...] = jnp.zeros(hl_ref.shape, MXU_DTYPE)
    w_ref[...] = jnp.zeros(w_ref.shape, MXU_DTYPE)
    zero_col = row_pos.astype(F32) * 0.0
    carry = (jnp.zeros((tq, LANES), F32), zero_col, zero_col, zero_col, zero_col)
    pairs_diag = (n_diag + 1) // 2
    pairs_full = jnp.maximum(n_blocks // 2, pairs_diag)
    pairs_valid = (n_blocks + 1) // 2

    def weights_vanished(carry):
        return (jnp.maximum(jnp.max(carry[1]), jnp.max(carry[2])) < SB_VANISH_LOG).astype(I32)

    def unmasked_pair(state):
        p, _, carry = state
        carry = pair_step(False, p, carry)
        return p + 1, weights_vanished(carry), carry

    carry = lax.fori_loop(0, pairs_diag, functools.partial(pair_step, True), carry)
    p, vanished, carry = lax.while_loop(lambda s: (s[0] < pairs_full) & (s[1] == 0), unmasked_pair,
                                        (pairs_diag, weights_vanished(carry), carry))
    p_end = jnp.where(vanished == 1, p, jnp.maximum(pairs_valid, p))
    carry = lax.fori_loop(p, p_end, functools.partial(pair_step, True), carry)
    carry = half_step(False, 2 * p_end, 0, carry, stages=(2, 3))
    carry = half_step(False, 2 * p_end + 1, 1, carry, stages=(3,))
    o_ref[...] = carry[0].astype(o_ref.dtype)


def _stick_breaking(q, k, v, tri, q_off):
    b, t, _ = q.shape
    lp = k.shape[1]
    tq = min(SB_Q_TILE, t)
    assert t % tq == 0 and lp % SB_KEY_TILE == 0
    pairs = WIDTH // LANES
    kv_spec = pl.BlockSpec((None, lp, LANES), lambda bi, pi, qi: (bi, 0, pi))
    q_spec = pl.BlockSpec((None, tq, LANES), lambda bi, pi, qi: (bi, qi, pi))
    return pl.pallas_call(
        functools.partial(_stick_breaking_kernel, q_off=q_off, tq=tq),
        grid=(b, pairs, t // tq),
        in_specs=[q_spec, kv_spec, kv_spec, pl.BlockSpec(tri.shape, lambda bi, pi, qi: (0, 0))],
        out_specs=q_spec,
        out_shape=jax.ShapeDtypeStruct(q.shape, MXU_DTYPE),
        scratch_shapes=[pltpu.VMEM((2, tq, 2 * LANES), F32),
                        pltpu.VMEM((2, tq, 4 * LANES), MXU_DTYPE),
                        pltpu.VMEM((2, tq, 2 * LANES), MXU_DTYPE)],
        compiler_params=_params(("parallel", "parallel", "arbitrary")),
        name="stick_breaking",
    )(q, k, v, tri)


def _bias_tiles_kernel(rb_ref, bucket_ref, tb_ref):
    h = pl.program_id(0)
    tiles = []
    for d in range(bucket_ref.shape[0]):
        bucket = bucket_ref[d]
        acc = jnp.zeros(bucket.shape, F32)
        for j in range(N_BUCKETS):
            acc = jnp.where(bucket == j, rb_ref[j, h], acc)
        tiles.append(acc)
    for d, tile in enumerate(tiles):
        tb_ref[d] = (tile - tiles[-1]) * LOG2E


def _t5_bucket(rel):
    half = N_BUCKETS // 2
    max_exact = half // 2
    base = jnp.where(rel > 0, half, 0)
    n = jnp.abs(rel)
    large = max_exact + (jnp.log(jnp.maximum(n, 1).astype(jnp.float32) / max_exact)
                         / math.log(T5_MAX_DISTANCE / max_exact) * (half - max_exact)).astype(jnp.int32)
    large = jnp.minimum(large, half - 1)
    return base + jnp.where(n < max_exact, n, large)


N_BIAS_TILES = 3
N_NEAR_TILES = 2


def _bias_tiles(rel_bias):
    assert T5_MAX_DISTANCE <= LANES
    j = jnp.arange(LANES, dtype=I32)[None, :, None]
    i = jnp.arange(LANES, dtype=I32)[None, None, :]
    d = jnp.arange(N_BIAS_TILES, dtype=I32)[:, None, None]
    bucket = _t5_bucket(j - i - LANES * d).astype(I32)
    return pl.pallas_call(
        _bias_tiles_kernel,
        grid=(N_HEADS,),
        in_specs=[pl.BlockSpec(memory_space=pltpu.SMEM),
                  pl.BlockSpec(bucket.shape, lambda h: (0, 0, 0))],
        out_specs=pl.BlockSpec((N_BIAS_TILES, LANES, LANES), lambda h: (h, 0, 0)),
        out_shape=jax.ShapeDtypeStruct((N_HEADS * N_BIAS_TILES, LANES, LANES), F32),
        compiler_params=_params(("arbitrary",)),
        name="bias_tiles",
    )(rel_bias, bucket)


def _sparse_kernel(q_ref, k_ref, vt_ref, qi_ref, kid_ref, wit_ref, tb_ref, ot_ref,
                   key_ref, mb_ref, cut_ref, *, q_off, tq, n_keys, top_k):
    kt_w = KEY_TILE
    p0 = q_off + pl.program_id(1) * tq
    lane = lax.broadcasted_iota(I32, (1, LANES), 1)
    key_row = lax.broadcasted_iota(I32, (kt_w, tq), 0)
    q_chunk = lax.shift_right_arithmetic(p0 + lax.broadcasted_iota(I32, (1, tq), 1), CHUNK_SHIFT)
    adm_end = (lax.shift_right_arithmetic(p0 + tq - 1, CHUNK_SHIFT) + 1) * CHUNK
    n_blocks = (jnp.minimum(adm_end, n_keys) + kt_w - 1) // kt_w
    half_masks = [lane < HEAD_DIM, lane >= HEAD_DIM]

    def admissible(start):
        pos = start + key_row
        return (lax.shift_right_arithmetic(pos, CHUNK_SHIFT) <= q_chunk) & (pos < n_keys)

    wit = wit_ref[...]
    qi_heads = []
    for h in range(N_HEADS):
        pair = qi_ref[:, (h // 2) * LANES:(h // 2 + 1) * LANES]
        qi_heads.append(jnp.where(half_masks[h % 2], pair, 0))

    def score_block(kb, _):
        start = pl.multiple_of(kb * kt_w, kt_w)
        kt = kid_ref[pl.ds(start, kt_w), :]
        acc = jnp.zeros((kt_w, tq), F32)
        for h in range(N_HEADS):
            acc = acc + jnp.maximum(_dot_nt(kt, qi_heads[h]), 0.0) * wit[h:h + 1, :]
        acc = jnp.where(acc == 0.0, 0.0, acc)
        bits = lax.bitcast_convert_type(acc, I32)
        key = jnp.where(bits < 0, bits ^ 0x7FFFFFFF, bits)
        key_ref[kb] = jnp.where(admissible(start), key, INT_MIN)
        return 0

    lax.fori_loop(0, n_blocks, score_block, 0)

    def count(pred):
        def blk(kb, cnt):
            for r in range(0, kt_w, COUNT_ROWS):
                hit = jnp.where(pred(key_ref[kb, r:r + COUNT_ROWS, :], kb * kt_w + r), 1.0, 0.0)
                parts = [hit[i * SUBLANES:(i + 1) * SUBLANES, :] for i in range(COUNT_ROWS // SUBLANES)]
                while len(parts) > 1:
                    parts = [a + b for a, b in zip(parts[::2], parts[1::2])]
                cnt = cnt + parts[0]
            return cnt
        cnt = lax.fori_loop(0, n_blocks, blk, jnp.zeros((SUBLANES, tq), F32))
        return jnp.sum(cnt, axis=0, keepdims=True)

    k_f = float(top_k)
    thr = jnp.where(count(lambda key, row0: key >= 0) >= k_f, 0, INT_MIN).astype(I32)

    def bit_step(i, thr):
        cand = thr | lax.shift_left(jnp.int32(1), 30 - i)
        return jnp.where(count(lambda key, row0: key >= cand) >= k_f, cand, thr)

    thr = lax.fori_loop(0, 31, bit_step, thr)

    n_ge = count(lambda key, row0: key >= thr)
    n_cols = key_ref.shape[0] * kt_w
    cut_ref[...] = jnp.full(cut_ref.shape, n_cols, I32)

    @pl.when(jnp.max(n_ge) > k_f)
    def _():
        need = k_f - count(lambda key, row0: key > thr)
        chunk_row = lax.broadcasted_iota(I32, (COUNT_ROWS, tq), 0)

        def cut_step(i, cut):
            cand = cut | lax.shift_left(jnp.int32(1), n_cols.bit_length() - 1 - i)
            n_eq = count(lambda key, row0: (key == thr) & ((row0 + chunk_row) < cand))
            return jnp.where(n_eq <= need, cand, cut)

        cut = lax.fori_loop(0, n_cols.bit_length(), cut_step, jnp.zeros((1, tq), I32))
        cut_ref[...] = jnp.broadcast_to(cut, cut_ref.shape)

    cut = cut_ref[0:1, :]

    def mask_block(kb, _):
        start = kb * kt_w
        key = key_ref[kb]
        sel = (key > thr) | ((key == thr) & ((start + key_row) < cut))
        mb_ref[kb] = jnp.where(sel & admissible(start), 0.0, NEG)
        return 0

    lax.fori_loop(0, n_blocks, mask_block, 0)

    n_pairs = WIDTH // LANES
    first_half = half_masks[0]
    top_rows = lax.broadcasted_iota(I32, (LANES, 1), 0) < HEAD_DIM

    def attend(near, kb, carry):
        start = pl.multiple_of(kb * kt_w, kt_w)
        mb = mb_ref[kb]
        q_groups = [(c, min(LANES, tq - c)) for c in range(0, tq, LANES)]
        tile_d = [[jnp.clip((p0 + c - (start + sub * LANES)) // LANES, 0, N_BIAS_TILES - 1) for c, _ in q_groups]
                  for sub in range(kt_w // LANES)]
        def pair_scores(pair):
            lanes = slice(pair * LANES, (pair + 1) * LANES)
            k_bd = _head_block_diag(k_ref[pl.ds(start, kt_w), lanes], first_half)
            return _dot_nt(k_bd, q_ref[:, lanes])

        scores = {pair: pair_scores(pair) for pair in range(n_pairs)}
        out = []
        for pair in range(n_pairs):
            lanes = slice(pair * LANES, (pair + 1) * LANES)
            m, l, acc = carry[pair]
            vt = vt_ref[lanes, pl.ds(start, kt_w)]
            vt_bd = jnp.concatenate([jnp.where(top_rows, vt, 0), jnp.where(top_rows, 0, vt)], axis=1)
            z = scores.pop(pair)
            m_new, alpha, p = [], [], []
            for hh in range(2):
                head = 2 * pair + hh
                s = z[hh * kt_w:(hh + 1) * kt_w, :] + mb
                if near:
                    bias = [jnp.concatenate([tb_ref[head * N_BIAS_TILES + d][:, :w] for d, (_, w) in zip(ds, q_groups)],
                                            axis=1) for ds in tile_d]
                    s = s + jnp.concatenate(bias, axis=0)
                m_h = jnp.maximum(m[hh], jnp.max(s, axis=0, keepdims=True))
                a_h = jnp.exp2(m[hh] - m_h)
                p_h = jnp.exp2(s - m_h)
                l = l[:hh] + (a_h * l[hh] + jnp.sum(p_h, axis=0, keepdims=True),) + l[hh + 1:]
                m_new.append(m_h)
                alpha.append(a_h)
                p.append(p_h.astype(MXU_DTYPE))
            acc = jnp.where(top_rows, alpha[0], alpha[1]) * acc + _dot(vt_bd, jnp.concatenate(p, axis=0))
            out.append((tuple(m_new), l, acc))
        return tuple(out)

    neg_row = jnp.full((1, tq), NEG, F32)
    zero_row = jnp.zeros((1, tq), F32)
    init = tuple(((neg_row, neg_row), (zero_row, zero_row), jnp.zeros((LANES, tq), F32)) for _ in range(n_pairs))
    n_far = jnp.maximum(n_blocks - N_NEAR_TILES, 0)
    carry = lax.fori_loop(0, n_far, functools.partial(attend, False), init)
    final = lax.fori_loop(n_far, n_blocks, functools.partial(attend, True), carry)
    for pair in range(n_pairs):
        _, l, acc = final[pair]
        ot_ref[pair * LANES:(pair + 1) * LANES, :] = (acc / jnp.where(top_rows, l[0], l[1])).astype(ot_ref.dtype)


def _sparse_attention(q, k, v, qi, kid, wi, tiles, q_off, n_keys):
    b, t, _ = q.shape
    lp = k.shape[1]
    tq = min(Q_TILE, t)
    assert t % tq == 0 and lp % KEY_TILE == 0 and q_off % LANES == 0 and (tq % LANES == 0 or t == tq)
    assert LANES + tq <= KEY_TILE
    top_k = min(TOPK_MAX, n_keys // 4)
    n_blocks = lp // KEY_TILE
    vt = v.transpose(0, 2, 1)
    wit = wi[:, :, :N_HEADS].transpose(0, 2, 1)
    q_spec = lambda w: pl.BlockSpec((None, tq, w), lambda bi, qi_: (bi, qi_, 0))
    k_spec = lambda w: pl.BlockSpec((None, lp, w), lambda bi, qi_: (bi, 0, 0))
    qt_spec = lambda rows: pl.BlockSpec((None, rows, tq), lambda bi, qi_: (bi, 0, qi_))
    out_t = pl.pallas_call(
        functools.partial(_sparse_kernel, q_off=q_off, tq=tq, n_keys=n_keys, top_k=top_k),
        grid=(b, t // tq),
        in_specs=[q_spec(WIDTH), k_spec(WIDTH), pl.BlockSpec((None, WIDTH, lp), lambda bi, qi_: (bi, 0, 0)),
                  q_spec(WIDTH), k_spec(LANES), qt_spec(N_HEADS),
                  pl.BlockSpec(tiles.shape, lambda bi, qi_: (0, 0, 0))],
        out_specs=qt_spec(WIDTH),
        out_shape=jax.ShapeDtypeStruct((b, WIDTH, t), MXU_DTYPE),
        scratch_shapes=[pltpu.VMEM((n_blocks, KEY_TILE, tq), I32),
                        pltpu.VMEM((n_blocks, KEY_TILE, tq), F32),
                        pltpu.VMEM((SUBLANES, tq), I32)],
        compiler_params=_params(("parallel", "arbitrary")),
        name="sparse_attention",
    )(q, k, vt, qi, kid, wit, tiles)
    return out_t.transpose(0, 2, 1)


def _ffn_kernel(x_ref, oa_ref, ob_ref, wo_ref, g2_ref, wup_ref, cw_ref, cb_ref, wdn_ref, st_ref,
                y_ref, cs_ref, x2_ref, h2_ref, acc_ref, prev_ref, u_ref, *, tm, n_chunks):
    keep = CONV_WIDTH - 1

    @pl.when(pl.program_id(1) == 0)
    def _():
        prev_ref[:, :, SUBLANES - keep:, :] = st_ref[...]

    mix = jnp.concatenate([oa_ref[...], ob_ref[...]], axis=1)
    x2 = x_ref[...] + _dot(mix, wo_ref[...])
    x2_ref[...] = x2
    ms = jnp.mean(x2 * x2, axis=-1, keepdims=True)
    h2_ref[...] = (x2 * lax.rsqrt(ms + EPS) * g2_ref[...]).astype(MXU_DTYPE)
    acc_ref[...] = jnp.zeros(acc_ref.shape, F32)
    row = lax.broadcasted_iota(I32, (tm, 1), 0)

    def up_proj(c, slot):
        h2 = h2_ref[...]
        for half in range(2):
            u_ref[slot, half] = _dot(h2, wup_ref[half, c])

    def finish_chunk(c, slot):
        conv = []
        for half in range(2):
            u = u_ref[slot, half]
            prev = prev_ref[half, c]
            before2, before1 = prev[SUBLANES - 2:SUBLANES - 1, :], prev[SUBLANES - 1:, :]
            u1 = jnp.where(row == 0, before1, pltpu.roll(u, 1, 0))
            u2 = jnp.where(row == 0, before2, jnp.where(row == 1, before1, pltpu.roll(u, 2, 0)))
            w = cw_ref[half, c]
            conv.append(cb_ref[half, c] + u2 * w[0:1, :] + u1 * w[1:2, :] + u * w[2:3, :])
            prev_ref[half, c] = u[tm - SUBLANES:, :]
            cs_ref[half, c] = u[tm - keep:, :]
        a, g = conv
        act = g * (1.0 / (1.0 + jnp.exp(-g))) * a
        acc_ref[...] += _dot(act.astype(MXU_DTYPE), wdn_ref[c])

    def chunk_pair(p, _):
        c = 2 * p
        up_proj(c + 1, 1)
        finish_chunk(c, 0)
        up_proj(c + 2, 0)
        finish_chunk(c + 1, 1)
        return 0

    up_proj(0, 0)
    n_pairs = (n_chunks - 1) // 2
    lax.fori_loop(0, n_pairs, chunk_pair, 0)
    last = 2 * n_pairs
    if last + 1 < n_chunks:
        up_proj(last + 1, 1)
        finish_chunk(last, 0)
        finish_chunk(last + 1, 1)
    else:
        finish_chunk(last, 0)
    y_ref[...] = x2_ref[...] + acc_ref[...]


def _ffn(x, oa, ob, w_out, g2, w_up, conv_w, conv_b, w_down, state):
    b, t, d = x.shape
    tm = min(ROW_TILE, t)
    assert t % tm == 0 and tm >= SUBLANES and CONV_WIDTH == 3
    _, n_chunks, _, fc = w_up.shape
    full = lambda a: pl.BlockSpec(a.shape, lambda bi, ti: (0,) * a.ndim)
    row_spec = lambda w: pl.BlockSpec((None, tm, w), lambda bi, ti: (bi, ti, 0))
    st_spec = pl.BlockSpec((None,) + state.shape[1:], lambda bi, ti: (bi, 0, 0, 0, 0))
    return pl.pallas_call(
        functools.partial(_ffn_kernel, tm=tm, n_chunks=n_chunks),
        grid=(b, t // tm),
        in_specs=[row_spec(d), row_spec(WIDTH), row_spec(WIDTH), full(w_out), full(g2), full(w_up),
                  full(conv_w), full(conv_b), full(w_down), st_spec],
        out_specs=[row_spec(d), st_spec],
        out_shape=[jax.ShapeDtypeStruct(x.shape, F32), jax.ShapeDtypeStruct(state.shape, F32)],
        scratch_shapes=[pltpu.VMEM((tm, d), F32), pltpu.VMEM((tm, d), MXU_DTYPE), pltpu.VMEM((tm, d), F32),
                        pltpu.VMEM((2, n_chunks, SUBLANES, fc), F32),
                        pltpu.VMEM((2, 2, tm, fc), F32)],
        compiler_params=_params(("parallel", "arbitrary")),
        name="ffn",
    )(x, oa, ob, w_out, g2, w_up, conv_w, conv_b, w_down, state)


def _pad_keys(a, lp):
    return jnp.pad(a, ((0, 0), (0, lp - a.shape[1]), (0, 0)))


def _state_to_chunks(s, fc):
    b, keep, f2 = s.shape
    return s.reshape(b, keep, 2, f2 // 2 // fc, fc).transpose(0, 2, 3, 1, 4)


def _state_from_chunks(s):
    b, _, n_chunks, keep, fc = s.shape
    return s.transpose(0, 3, 1, 2, 4).reshape(b, keep, 2 * n_chunks * fc)


def _layer(x, caches, lw, consts, layer, depth, state_rows):
    b, t, d = x.shape
    tri, seg, tiles = consts
    outs = _in_proj(x.reshape(b * t, d), lw["g1"], lw["w_main"], lw["w_tail"], lw["gq"], lw["gk"], seg,
                    layer, depth, state_rows, t)
    state_rows = outs[:N_STATE_OUTPUTS]
    qa, ka, va, qb, kb, vb, qi, kid, wi = [o.reshape(b, t, -1) for o in outs[N_STATE_OUTPUTS:]]
    fc = lw["w_up"].shape[-1]
    if caches is None:
        past = 0
        state = jnp.zeros((b, 2, lw["w_up"].shape[1], CONV_WIDTH - 1, fc), F32)
    else:
        c_ak, c_av, c_bk, c_bv, c_ik, c_conv = caches
        past = c_ak.shape[1]
        flat = lambda c: c.reshape(b, past, -1).astype(MXU_DTYPE)
        ka = jnp.concatenate([flat(c_ak), ka], axis=1)
        va = jnp.concatenate([flat(c_av), va], axis=1)
        kb = jnp.concatenate([flat(c_bk), kb], axis=1)
        vb = jnp.concatenate([flat(c_bv), vb], axis=1)
        kid = jnp.concatenate([jnp.concatenate([flat(c_ik)] * 2, axis=-1), kid], axis=1)
        state = _state_to_chunks(c_conv, fc)
    n_keys = past + t
    lp = -(-n_keys // KEY_TILE) * KEY_TILE
    ka, va, kb, vb, kid = [_pad_keys(a, lp) for a in (ka, va, kb, vb, kid)]
    oa = _stick_breaking(qa, ka, va, tri, past)
    ob = _sparse_attention(qb, kb, vb, qi, kid, wi, tiles, past, n_keys)
    y, new_state = _ffn(x, oa, ob, lw["w_out"], lw["g2"], lw["w_up"], lw["conv_w"], lw["conv_b"],
                        lw["w_down"], state)
    return y, state_rows, _state_from_chunks(new_state)


def _layer_weights(l, norm1, w_in, q_norm, k_norm, w_out, norm2, w_up, conv_w, conv_b, w_down):
    d = w_in.shape[1]
    f = w_down.shape[1]
    fc = FF_TILE if f % FF_TILE == 0 else LANES
    assert f % fc == 0 and w_in.shape[2] == 7 * WIDTH + IDX_DIM + N_HEADS
    n_chunks = f // fc
    w = w_in[l]
    main = 7 * WIDTH
    w_ik, w_iw = w[:, main:main + IDX_DIM], w[:, main + IDX_DIM:]
    w_tail = jnp.concatenate([w_ik, w_ik, w_iw, jnp.zeros((d, LANES - N_HEADS), w.dtype)], axis=1)
    return {
        "g1": norm1[l][None, :],
        "w_main": w[:, :main].astype(MXU_DTYPE),
        "w_tail": w_tail.astype(MXU_DTYPE),
        "gq": jnp.tile(q_norm[l], N_HEADS)[None, :],
        "gk": jnp.tile(k_norm[l], N_HEADS)[None, :],
        "w_out": w_out[l].astype(MXU_DTYPE),
        "g2": norm2[l][None, :],
        "w_up": w_up[l].astype(MXU_DTYPE).reshape(d, 2, n_chunks, fc).transpose(1, 2, 0, 3),
        "conv_w": conv_w[l].reshape(CONV_WIDTH, 2, n_chunks, fc).transpose(1, 2, 0, 3),
        "conv_b": conv_b[l].reshape(2, n_chunks, 1, fc),
        "w_down": w_down[l].astype(MXU_DTYPE).reshape(n_chunks, fc, d),
    }


def kernel(x_prompt, x_sample, cache_a_k, cache_a_v, cache_b_k, cache_b_v, cache_idx_k, state_ffn_conv,
           rel_bias, norm1, w_in, q_norm, k_norm, w_out, norm2, w_up, conv_w, conv_b, w_down):
    depth = w_in.shape[0]
    assert cache_a_k.shape[3:] == (N_HEADS, HEAD_DIM) and cache_idx_k.shape[-1] == IDX_DIM
    r2 = jnp.arange(2 * LANES, dtype=I32)
    tri = ((r2[:, None] > r2[None, :]) & (r2[:, None] // LANES == r2[None, :] // LANES)).astype(MXU_DTYPE)
    tri = jnp.concatenate([tri, tri], axis=0)
    g = jnp.arange(WIDTH, dtype=I32) // HEAD_DIM
    seg = (g[:, None] == g[None, :]).astype(MXU_DTYPE)
    seg = jnp.concatenate([seg, seg], axis=0)
    consts = (tri, seg, _bias_tiles(rel_bias))

    yp, ys = x_prompt, x_sample
    p_rows = s_rows = None
    p_conv, s_conv = [], []
    for l in range(depth):
        lw = _layer_weights(l, norm1, w_in, q_norm, k_norm, w_out, norm2, w_up, conv_w, conv_b, w_down)
        yp, p_rows, conv = _layer(yp, None, lw, consts, l, depth, p_rows)
        p_conv.append(conv)
        caches = (cache_a_k[l], cache_a_v[l], cache_b_k[l], cache_b_v[l], cache_idx_k[l], state_ffn_conv[l])
        ys, s_rows, conv = _layer(ys, caches, lw, consts, l, depth, s_rows)
        s_conv.append(conv)

    def states(x, rows, conv):
        b, t, _ = x.shape
        heads = [a.reshape(depth, b, t, N_HEADS, HEAD_DIM) for a in rows[:4]]
        return tuple(heads) + (rows[4].reshape(depth, b, t, IDX_DIM), jnp.stack(conv, axis=0))

    return (yp, ys) + states(x_prompt, p_rows, p_conv) + states(x_sample, s_rows, s_conv)
```
